```python
import jax
import jax.numpy as jnp
from jax import lax
import numpy as np


D_MODEL = 2048
BATCH = 2
SEQ = 4096
DEPTH = 4

POOL_WIDTH = D_MODEL // 4
POOL_WINDOWS = (2, 4, 8, 16)
POOL_GROUPS = len(POOL_WINDOWS)
POOL_GROUP_DIM = POOL_WIDTH // POOL_GROUPS

GLA_WIDTH = (3 * D_MODEL) // 8
GLA_HEADS = 4
GLA_QK_WIDTH = GLA_WIDTH // 2
GLA_DK = GLA_QK_WIDTH // GLA_HEADS
GLA_DV = GLA_WIDTH // GLA_HEADS
GLA_LORA = 16
GLA_TAU = 16.0
GLA_CHUNK = 64

RWKV_WIDTH = D_MODEL - POOL_WIDTH - GLA_WIDTH
RWKV_HEAD_DIM = 64
RWKV_HEADS = RWKV_WIDTH // RWKV_HEAD_DIM
RWKV_DECAY_LORA = 64
RWKV_A_LORA = 64
RWKV_G_LORA = 128
RWKV_LNX_EPS = 64e-5

GLA_SPLITS = (GLA_QK_WIDTH, GLA_QK_WIDTH, GLA_WIDTH, GLA_WIDTH, GLA_LORA)
RWKV_SPLITS = (RWKV_WIDTH, RWKV_WIDTH, RWKV_WIDTH, RWKV_DECAY_LORA, RWKV_A_LORA, RWKV_G_LORA)
GLA_IN = sum(GLA_SPLITS)
RWKV_IN = sum(RWKV_SPLITS)
IN_WIDTH = POOL_WIDTH + GLA_IN + RWKV_IN

PEER_HEADS = 8
PEER_QDIM = 256
PEER_NKEYS = 128
PEER_N = PEER_NKEYS * PEER_NKEYS
PEER_TOPK = 16
PEER_TOKEN_BLOCK = 128

NORM_EPS = 1e-6

kernel_name = 'hybrid_pool_gla_rwkv7_peer_adaln'


def _split(t, sizes):
    idx = [int(i) for i in np.cumsum(sizes)[:-1]]
    return jnp.split(t, idx, axis=-1)


def rmsnorm(x, g):
    x32 = x.astype(jnp.float32)
    y = x32 * lax.rsqrt(jnp.mean(x32 * x32, axis=-1, keepdims=True) + NORM_EPS)
    return (y * g.astype(jnp.float32)).astype(x.dtype)


def causal_shift(t):
    return jnp.pad(t, ((0, 0), (1, 0), (0, 0)))[:, :-1]


def pool_group(p, pool_w, pool_scale):
    B, S, _ = p.shape
    p32 = p.astype(jnp.float32)
    cs = jnp.cumsum(p32, axis=1)
    t = jnp.arange(1, S + 1, dtype=jnp.float32)
    outs = []
    for g, win in enumerate(POOL_WINDOWS):
        lo, hi = g * POOL_GROUP_DIM, (g + 1) * POOL_GROUP_DIM
        csg = cs[..., lo:hi]
        prev = jnp.pad(csg, ((0, 0), (win, 0), (0, 0)))[:, :S]
        cnt = jnp.minimum(t, float(win))[None, :, None]
        outs.append((csg - prev) / cnt - p32[..., lo:hi])
    d = jnp.stack(outs, axis=2)
    y = jnp.einsum('bsgc,gcd->bsgd', d, pool_w.astype(jnp.float32)).reshape(B, S, POOL_WIDTH)
    return y * pool_scale


def gla_group(q, k, v, g, a_down, alpha_up, alpha_b, norm_g):
    B, S, _ = q.shape
    f32 = jnp.float32
    hs = lambda t, d: t.astype(f32).reshape(B, S, GLA_HEADS, d)
    q = hs(q, GLA_DK) * (GLA_DK ** -0.5)
    k = hs(k, GLA_DK)
    v = hs(v, GLA_DV)
    log_alpha = hs(jax.nn.log_sigmoid(a_down.astype(f32) @ alpha_up + alpha_b) / GLA_TAU, GLA_DK)
    n_chunks = S // GLA_CHUNK
    chunks = lambda t: t.reshape(B, n_chunks, GLA_CHUNK, GLA_HEADS, -1).transpose(1, 0, 3, 2, 4)
    causal = jnp.tril(jnp.ones((GLA_CHUNK, GLA_CHUNK), dtype=bool))[:, :, None]

    def step(state, inp):
        qc, kc, vc, lac = inp
        b = jnp.cumsum(lac, axis=2)
        inter = jnp.einsum('bhcd,bhde->bhce', qc * jnp.exp(b), state)
        rel = jnp.where(causal, b[:, :, :, None, :] - b[:, :, None, :, :], -jnp.inf)
        att = jnp.einsum('bhid,bhjd,bhijd->bhij', qc, kc, jnp.exp(rel))
        out = inter + jnp.einsum('bhij,bhje->bhie', att, vc)
        b_last = b[:, :, -1:, :]
        state = (jnp.exp(b_last[:, :, 0, :])[..., None] * state
                 + jnp.einsum('bhcd,bhce->bhde', kc * jnp.exp(b_last - b), vc))
        return state, out

    s0 = jnp.zeros((B, GLA_HEADS, GLA_DK, GLA_DV), f32)
    _, o = lax.scan(step, s0, (chunks(q), chunks(k), chunks(v), chunks(log_alpha)))
    o = o.transpose(1, 0, 3, 2, 4).reshape(B, S, GLA_HEADS, GLA_DV)
    o = o * lax.rsqrt(jnp.mean(o * o, axis=-1, keepdims=True) + NORM_EPS)
    return o.reshape(B, S, GLA_WIDTH) * norm_g * jax.nn.silu(g.astype(f32))


def rwkv7_group(y, mu, w0, w2, a0, a2, g2, k_k, k_a, r_k, lnx_g, lnx_b):
    B, S, _ = y.shape
    y = y.astype(jnp.float32)
    y = y + (causal_shift(y) - y) * mu
    r, k, v, wd, ad, gd = _split(y, RWKV_SPLITS)
    w = -jax.nn.softplus(-(w0 + jnp.tanh(wd) @ w2)) - 0.5
    decay = jnp.exp(-jnp.exp(w))
    a = jax.nn.sigmoid(a0 + ad @ a2)
    gate = jax.nn.sigmoid(gd) @ g2
    hs = lambda t: t.reshape(B, S, RWKV_HEADS, RWKV_HEAD_DIM)
    kk = hs(k * k_k)
    kk = kk / jnp.maximum(jnp.sqrt(jnp.sum(kk * kk, axis=-1, keepdims=True)), 1e-12)
    k = k * (1.0 + (a - 1.0) * k_a)
    r, k, v, decay, a = hs(r), hs(k), hs(v), hs(decay), hs(a)
    vec_a = -kk
    vec_b = kk * a

    def step(state, inp):
        r_t, w_t, k_t, v_t, a_t, b_t = inp
        sa = jnp.einsum('bhvk,bhk->bhv', state, a_t)
        state = (state * w_t[:, :, None, :] + sa[..., None] * b_t[:, :, None, :]
                 + v_t[..., None] * k_t[:, :, None, :])
        return state, jnp.einsum('bhvk,bhk->bhv', state, r_t)

    tm = lambda t: jnp.swapaxes(t, 0, 1)
    s0 = jnp.zeros((B, RWKV_HEADS, RWKV_HEAD_DIM, RWKV_HEAD_DIM), jnp.float32)
    _, o = lax.scan(step, s0, (tm(r), tm(decay), tm(k), tm(v), tm(vec_a), tm(vec_b)))
    o = tm(o)
    mean = jnp.mean(o, axis=-1, keepdims=True)
    var = jnp.mean((o - mean) ** 2, axis=-1, keepdims=True)
    o = ((o - mean) * lax.rsqrt(var + RWKV_LNX_EPS)).reshape(B, S, RWKV_WIDTH) * lnx_g + lnx_b
    bonus = jnp.sum(r * k * r_k, axis=-1, keepdims=True) * v
    return (o + bonus.reshape(B, S, RWKV_WIDTH)) * gate


def hybrid_mixer(h, w_in, pool_w, pool_scale, gla_alpha_up, gla_alpha_b, gla_norm_g,
                 rwkv_mu, rwkv_w0, rwkv_w2, rwkv_a0, rwkv_a2, rwkv_g2, rwkv_kk, rwkv_ka, rwkv_rk,
                 rwkv_lnx_g, rwkv_lnx_b, w_out):
    proj = h @ w_in
    p_in, g_in, r_in = _split(proj, (POOL_WIDTH, GLA_IN, RWKV_IN))
    y_pool = pool_group(p_in, pool_w, pool_scale)
    q, k, v, g, a_down = _split(g_in, GLA_SPLITS)
    y_gla = gla_group(q, k, v, g, a_down, gla_alpha_up, gla_alpha_b, gla_norm_g)
    y_rwkv = rwkv7_group(r_in, rwkv_mu, rwkv_w0, rwkv_w2, rwkv_a0, rwkv_a2, rwkv_g2,
                         rwkv_kk, rwkv_ka, rwkv_rk, rwkv_lnx_g, rwkv_lnx_b)
    y = jnp.concatenate([y_pool, y_gla, y_rwkv], axis=-1).astype(h.dtype)
    return y @ w_out


def peer_ffn(h, wq, k1, k2, u_tab, v_tab):
    B, S, D = h.shape
    T = B * S
    ht = h.reshape(T, D)
    q = (ht @ wq).astype(jnp.float32).reshape(T, PEER_HEADS, 2, PEER_QDIM // 2)
    s1 = jnp.einsum('thd,hnd->thn', q[:, :, 0], k1)
    s2 = jnp.einsum('thd,hnd->thn', q[:, :, 1], k2)
    v1, i1 = lax.top_k(s1, PEER_TOPK)
    v2, i2 = lax.top_k(s2, PEER_TOPK)
    cand = (v1[..., :, None] + v2[..., None, :]).reshape(T, PEER_HEADS, PEER_TOPK * PEER_TOPK)
    sc, ci = lax.top_k(cand, PEER_TOPK)
    e_idx = (jnp.take_along_axis(i1, ci // PEER_TOPK, axis=-1) * PEER_NKEYS
             + jnp.take_along_axis(i2, ci % PEER_TOPK, axis=-1))
    gate = jax.nn.softmax(sc, axis=-1)
    nb = T // PEER_TOKEN_BLOCK

    def block(args):
        hb, eb, gb = args
        act = jax.nn.gelu(jnp.einsum('thkd,td->thk', u_tab[eb], hb).astype(jnp.float32), approximate=False)
        coeff = (gb * act).astype(h.dtype)
        return jnp.einsum('thk,thkd->td', coeff, v_tab[eb]).astype(h.dtype)

    out = lax.map(block, (ht.reshape(nb, PEER_TOKEN_BLOCK, D),
                          e_idx.reshape(nb, PEER_TOKEN_BLOCK, PEER_HEADS, PEER_TOPK),
                          gate.reshape(nb, PEER_TOKEN_BLOCK, PEER_HEADS, PEER_TOPK)))
    return out.reshape(B, S, D)


def setup_inputs(seed: int = 0) -> dict:
    key = jax.random.key(seed)
    keys = iter(jax.random.split(key, 40))

    def nrm(shape, std):
        return std * jax.random.normal(next(keys), shape, jnp.float32)

    L, D = DEPTH, D_MODEL
    return {
        'x': nrm((BATCH, SEQ, D), 1.0),
        'c': nrm((BATCH, D), 1.0),
        'ada_w': nrm((L, D, 6 * D), 0.5 * D ** -0.5),
        'ada_b': nrm((L, 6 * D), 0.02),
        'norm1_g': 1.0 + nrm((L, D), 0.02),
        'w_in': nrm((L, D, IN_WIDTH), D ** -0.5),
        'pool_w': nrm((L, POOL_GROUPS, POOL_GROUP_DIM, POOL_GROUP_DIM), POOL_GROUP_DIM ** -0.5),
        'pool_scale': 1.0 + nrm((L, POOL_WIDTH), 0.1),
        'gla_alpha_up': nrm((L, GLA_LORA, GLA_QK_WIDTH), GLA_LORA ** -0.5),
        'gla_alpha_b': nrm((L, GLA_QK_WIDTH), 0.02),
        'gla_norm_g': 1.0 + nrm((L, GLA_WIDTH), 0.02),
        'rwkv_mu': jax.random.uniform(next(keys), (L, RWKV_IN), jnp.float32),
        'rwkv_w0': -0.5 + nrm((L, RWKV_WIDTH), 0.5),
        'rwkv_w2': nrm((L, RWKV_DECAY_LORA, RWKV_WIDTH), 0.5 * RWKV_DECAY_LORA ** -0.5),
        'rwkv_a0': nrm((L, RWKV_WIDTH), 0.1),
        'rwkv_a2': nrm((L, RWKV_A_LORA, RWKV_WIDTH), 0.5 * RWKV_A_LORA ** -0.5),
        'rwkv_g2': nrm((L, RWKV_G_LORA, RWKV_WIDTH), RWKV_G_LORA ** -0.5),
        'rwkv_kk': 0.85 + nrm((L, RWKV_WIDTH), 0.02),
        'rwkv_ka': 1.0 + nrm((L, RWKV_WIDTH), 0.02),
        'rwkv_rk': nrm((L, RWKV_HEADS, RWKV_HEAD_DIM), 0.1),
        'rwkv_lnx_g': 1.0 + nrm((L, RWKV_WIDTH), 0.02),
        'rwkv_lnx_b': nrm((L, RWKV_WIDTH), 0.02),
        'w_out': nrm((L, D, D), D ** -0.5),
        'norm2_g': 1.0 + nrm((L, D), 0.02),
        'peer_wq': nrm((L, D, PEER_HEADS * PEER_QDIM), D ** -0.5),
        'peer_k1': nrm((L, PEER_HEADS, PEER_NKEYS, PEER_QDIM // 2), (PEER_QDIM // 2) ** -0.5),
        'peer_k2': nrm((L, PEER_HEADS, PEER_NKEYS, PEER_QDIM // 2), (PEER_QDIM // 2) ** -0.5),
        'peer_u': nrm((L, PEER_N, D), D ** -0.5),
        'peer_v': nrm((L, PEER_N, D), 0.5),
        'final_g': 1.0 + nrm((D,), 0.02),
    }


def reference(x, c, ada_w, ada_b, norm1_g, w_in, pool_w, pool_scale, gla_alpha_up, gla_alpha_b,
              gla_norm_g, rwkv_mu, rwkv_w0, rwkv_w2, rwkv_a0, rwkv_a2, rwkv_g2, rwkv_kk, rwkv_ka,
              rwkv_rk, rwkv_lnx_g, rwkv_lnx_b, w_out, norm2_g, peer_wq, peer_k1, peer_k2, peer_u,
              peer_v, final_g):
    c_act = jax.nn.silu(c)
    for l in range(DEPTH):
        mod = (c_act @ ada_w[l] + ada_b[l])[:, None, :]
        sh1, sc1, g1, sh2, sc2, g2 = jnp.split(mod, 6, axis=-1)
        h = rmsnorm(x, norm1_g[l]) * (1.0 + sc1) + sh1
        x = x + g1 * hybrid_mixer(h, w_in[l], pool_w[l], pool_scale[l], gla_alpha_up[l], gla_alpha_b[l],
                                  gla_norm_g[l], rwkv_mu[l], rwkv_w0[l], rwkv_w2[l], rwkv_a0[l], rwkv_a2[l],
                                  rwkv_g2[l], rwkv_kk[l], rwkv_ka[l], rwkv_rk[l], rwkv_lnx_g[l],
                                  rwkv_lnx_b[l], w_out[l])
        h = rmsnorm(x, norm2_g[l]) * (1.0 + sc2) + sh2
        x = x + g2 * peer_ffn(h, peer_wq[l], peer_k1[l], peer_k2[l], peer_u[l], peer_v[l])
    return rmsnorm(x, final_g)
```

```python
import functools

import jax
import jax.numpy as jnp
import numpy as np
from jax import lax
from jax.experimental import pallas as pl
from jax.experimental.pallas import tpu as pltpu

F32 = jnp.float32
BF16 = jnp.bfloat16

LANES = 128
NORM_EPS = 1e-6

POOL_WINDOWS = (2, 4, 8, 16)
POOL_GROUP_DIM = 128
POOL_WIDTH = 512
POOL_HALO = 16

GLA_HEADS = 4
GLA_DK = 96
GLA_DV = 192
GLA_DK_PAD = 128
GLA_DV_PAD = 256
GLA_LORA = 16
GLA_TAU = 16.0
GLA_CHUNK = 64
GLA_SUB = 16
GLA_QK_PAD = GLA_HEADS * GLA_DK_PAD
GLA_V_PAD = GLA_HEADS * GLA_DV_PAD

RWKV_WIDTH = 768
RWKV_HEAD_DIM = 64
RWKV_PAIRS = RWKV_WIDTH // LANES
RWKV_IN = 2560
RWKV_CHUNK = 64
RWKV_LNX_EPS = 64e-5

COL_RWKV = 0
COL_POOL = 2560
COL_Q = 3072
COL_K = 3584
COL_V = 4096
COL_G = 5120
COL_AD = 6144
IN_PAD = 6272

PEER_HEADS = 8
PEER_NKEYS = 128
PEER_HALF = 128
PEER_TOPK = 16
PEER_N = PEER_NKEYS * PEER_NKEYS

NN = (((1,), (0,)), ((), ()))
NT = (((1,), (1,)), ((), ()))

VMEM_LIMIT = 56 * 1024 * 1024


def _cp(sem):
    return pltpu.CompilerParams(dimension_semantics=sem, vmem_limit_bytes=VMEM_LIMIT)


def _dot(a, b, dims=NN):
    return lax.dot_general(a.astype(BF16), b.astype(BF16), dims, preferred_element_type=F32)


def _split(x):
    hi = x.astype(BF16)
    lo = (x - hi.astype(F32)).astype(BF16)
    return hi, lo


def _dot_l2(a, b, dims=NN):
    hi, lo = _split(a)
    bb = b.astype(BF16)
    return (lax.dot_general(hi, bb, dims, preferred_element_type=F32)
            + lax.dot_general(lo, bb, dims, preferred_element_type=F32))


def _dot_r2(a, b):
    hi, lo = _split(b)
    ab = a.astype(BF16)
    return (lax.dot_general(ab, hi, NN, preferred_element_type=F32)
            + lax.dot_general(ab, lo, NN, preferred_element_type=F32))


def _iota(shape, dim):
    return lax.broadcasted_iota(jnp.int32, shape, dim)


def _tri_incl(n):
    return (_iota((n, n), 0) >= _iota((n, n), 1)).astype(BF16)


def _sigmoid(x):
    return 1.0 / (1.0 + jnp.exp(-x))


def _softplus(x):
    return jnp.maximum(x, 0.0) + jnp.log1p(jnp.exp(-jnp.abs(x)))


def _mod_kernel(c_ref, w_ref, b_ref, o_ref):
    c = c_ref[...]
    ca = c * _sigmoid(c)
    o_ref[...] = jnp.dot(ca, w_ref[...], precision=lax.Precision.HIGHEST,
                         preferred_element_type=F32) + b_ref[...]


def _modulation(c, ada_w, ada_b):
    L, D, N6 = ada_w.shape
    B = c.shape[0]
    tn = 1024
    return pl.pallas_call(
        _mod_kernel,
        grid=(L, N6 // tn),
        in_specs=[pl.BlockSpec((B, D), lambda l, j: (0, 0)),
                  pl.BlockSpec((None, D, tn), lambda l, j: (l, 0, j)),
                  pl.BlockSpec((None, 1, tn), lambda l, j: (l, 0, j))],
        out_specs=pl.BlockSpec((None, B, tn), lambda l, j: (l, 0, j)),
        out_shape=jax.ShapeDtypeStruct((L, B, N6), F32),
        compiler_params=_cp(("parallel", "parallel")),
        name="adaln_mod",
    )(c, ada_w, ada_b.reshape(L, 1, N6))


def _normmod_kernel(x_ref, g_ref, sh_ref, sc_ref, o_ref):
    x = x_ref[...]
    y = x * lax.rsqrt(jnp.mean(x * x, axis=-1, keepdims=True) + NORM_EPS) * g_ref[...]
    o_ref[...] = (y * (1.0 + sc_ref[...]) + sh_ref[...]).astype(o_ref.dtype)


def _normmod(x2, g, mod3, sh_blk, sc_blk, tiles_per_batch, tm):
    T, D = x2.shape
    return pl.pallas_call(
        _normmod_kernel,
        grid=(T // tm,),
        in_specs=[pl.BlockSpec((tm, D), lambda i: (i, 0)),
                  pl.BlockSpec((1, D), lambda i: (0, 0)),
                  pl.BlockSpec((None, 1, D), lambda i: (i // tiles_per_batch, 0, sh_blk)),
                  pl.BlockSpec((None, 1, D), lambda i: (i // tiles_per_batch, 0, sc_blk))],
        out_specs=pl.BlockSpec((tm, D), lambda i: (i, 0)),
        out_shape=jax.ShapeDtypeStruct((T, D), BF16),
        compiler_params=_cp(("parallel",)),
        name="rmsnorm_adaln",
    )(x2, g.reshape(1, D), mod3, mod3)


def _resid_normmod_kernel(x_ref, f_ref, gate_ref, g_ref, sh_ref, sc_ref, xo_ref, h_ref):
    x = x_ref[...] + gate_ref[...] * f_ref[...]
    xo_ref[...] = x
    y = x * lax.rsqrt(jnp.mean(x * x, axis=-1, keepdims=True) + NORM_EPS) * g_ref[...]
    h_ref[...] = (y * (1.0 + sc_ref[...]) + sh_ref[...]).astype(h_ref.dtype)


def _resid_normmod(x2, f, mod_prev3, gate_blk, g, mod3, sh_blk, sc_blk, tiles_per_batch, tm):
    T, D = x2.shape
    row = lambda blk: pl.BlockSpec((None, 1, D), lambda i: (i // tiles_per_batch, 0, blk))
    tile = pl.BlockSpec((tm, D), lambda i: (i, 0))
    return pl.pallas_call(
        _resid_normmod_kernel,
        grid=(T // tm,),
        in_specs=[tile, tile, row(gate_blk), pl.BlockSpec((1, D), lambda i: (0, 0)), row(sh_blk), row(sc_blk)],
        out_specs=[tile, tile],
        out_shape=[jax.ShapeDtypeStruct((T, D), F32), jax.ShapeDtypeStruct((T, D), BF16)],
        compiler_params=_cp(("parallel",)),
        name="residual_rmsnorm_adaln",
    )(x2, f, mod_prev3, g.reshape(1, D), mod3, mod3)


def _resid_final_kernel(x_ref, f_ref, gate_ref, g_ref, o_ref):
    x = x_ref[...] + gate_ref[...] * f_ref[...]
    o_ref[...] = x * lax.rsqrt(jnp.mean(x * x, axis=-1, keepdims=True) + NORM_EPS) * g_ref[...]


def _resid_final(x2, f, mod_prev3, gate_blk, g, tiles_per_batch, tm):
    T, D = x2.shape
    tile = pl.BlockSpec((tm, D), lambda i: (i, 0))
    return pl.pallas_call(
        _resid_final_kernel,
        grid=(T // tm,),
        in_specs=[tile, tile,
                  pl.BlockSpec((None, 1, D), lambda i: (i // tiles_per_batch, 0, gate_blk)),
                  pl.BlockSpec((1, D), lambda i: (0, 0))],
        out_specs=tile,
        out_shape=jax.ShapeDtypeStruct((T, D), F32),
        compiler_params=_cp(("parallel",)),
        name="residual_final_rmsnorm",
    )(x2, f, mod_prev3, g.reshape(1, D))


def _matmul_kernel(a_ref, w_ref, o_ref):
    o_ref[...] = jnp.dot(a_ref[...], w_ref[...], preferred_element_type=F32)


def _matmul(a, w, tm, tn):
    M, K = a.shape
    N = w.shape[1]
    return pl.pallas_call(
        _matmul_kernel,
        grid=(M // tm, N // tn),
        in_specs=[pl.BlockSpec((tm, K), lambda i, j: (i, 0)),
                  pl.BlockSpec((K, tn), lambda i, j: (0, j))],
        out_specs=pl.BlockSpec((tm, tn), lambda i, j: (i, j)),
        out_shape=jax.ShapeDtypeStruct((M, N), F32),
        compiler_params=_cp(("parallel", "parallel")),
        name="projection",
    )(a, w)


def _outproj_kernel(yp_ref, yg_ref, yr_ref, wp_ref, wg_ref, wr_ref, x_ref, gate_ref, o_ref):
    acc = jnp.dot(yp_ref[...], wp_ref[...], preferred_element_type=F32)
    acc += jnp.dot(yg_ref[...], wg_ref[...], preferred_element_type=F32)
    acc += jnp.dot(yr_ref[...], wr_ref[...], preferred_element_type=F32)
    o_ref[...] = x_ref[...] + gate_ref[...] * acc


def _outproj(yp, yg, yr, wp, wg, wr, x2, mod3, gate_blk0, tiles_per_batch, tm, tn):
    T, D = x2.shape
    return pl.pallas_call(
        _outproj_kernel,
        grid=(T // tm, D // tn),
        in_specs=[pl.BlockSpec((tm, yp.shape[1]), lambda i, j: (i, 0)),
                  pl.BlockSpec((tm, yg.shape[1]), lambda i, j: (i, 0)),
                  pl.BlockSpec((tm, yr.shape[1]), lambda i, j: (i, 0)),
                  pl.BlockSpec((wp.shape[0], tn), lambda i, j: (0, j)),
                  pl.BlockSpec((wg.shape[0], tn), lambda i, j: (0, j)),
                  pl.BlockSpec((wr.shape[0], tn), lambda i, j: (0, j)),
                  pl.BlockSpec((tm, tn), lambda i, j: (i, j)),
                  pl.BlockSpec((None, 1, tn), lambda i, j: (i // tiles_per_batch, 0, gate_blk0 + j))],
        out_specs=pl.BlockSpec((tm, tn), lambda i, j: (i, j)),
        out_shape=jax.ShapeDtypeStruct((T, D), F32),
        compiler_params=_cp(("parallel", "parallel")),
        name="out_projection",
    )(yp, yg, yr, wp, wg, wr, x2, mod3)


def _pool_kernel(p_ref, w_ref, scale_ref, o_ref, ext_ref, *, ts):
    i = pl.program_id(1)

    @pl.when(i == 0)
    def _():
        ext_ref[0:POOL_HALO, :] = jnp.zeros((POOL_HALO, POOL_WIDTH), F32)

    p = p_ref[...]
    ext_ref[POOL_HALO:POOL_HALO + ts, :] = p
    pos = (i * ts + _iota((ts, 1), 0) + 1).astype(F32)
    for g, win in enumerate(POOL_WINDOWS):
        lo = g * POOL_GROUP_DIM
        acc = p[:, lo:lo + POOL_GROUP_DIM]
        for j in range(1, win):
            acc = acc + ext_ref[POOL_HALO - j:POOL_HALO - j + ts, lo:lo + POOL_GROUP_DIM]
        d = acc / jnp.minimum(pos, float(win)) - p[:, lo:lo + POOL_GROUP_DIM]
        y = _dot(d, w_ref[g]) * scale_ref[:, lo:lo + POOL_GROUP_DIM]
        o_ref[:, lo:lo + POOL_GROUP_DIM] = y.astype(o_ref.dtype)
    ext_ref[0:POOL_HALO, :] = ext_ref[ts:ts + POOL_HALO, :]


def _pool(proj, pool_w, pool_scale, B, S, ts):
    T = B * S
    nt = S // ts
    return pl.pallas_call(
        functools.partial(_pool_kernel, ts=ts),
        grid=(B, nt),
        in_specs=[pl.BlockSpec((ts, POOL_WIDTH), lambda b, i: (b * nt + i, COL_POOL // POOL_WIDTH)),
                  pl.BlockSpec(pool_w.shape, lambda b, i: (0, 0, 0)),
                  pl.BlockSpec((1, POOL_WIDTH), lambda b, i: (0, 0))],
        out_specs=pl.BlockSpec((ts, POOL_WIDTH), lambda b, i: (b * nt + i, 0)),
        out_shape=jax.ShapeDtypeStruct((T, POOL_WIDTH), BF16),
        scratch_shapes=[pltpu.VMEM((ts + POOL_HALO, POOL_WIDTH), F32)],
        compiler_params=_cp(("arbitrary", "arbitrary")),
        name="causal_pool",
    )(proj, pool_w, pool_scale.reshape(1, POOL_WIDTH))


def _gla_kernel(q_ref, k_ref, v_ref, g_ref, ad_ref, up_ref, ab_ref, ng_ref, o_ref, st_ref):
    C, SUB = GLA_CHUNK, GLA_SUB

    @pl.when(pl.program_id(1) == 0)
    def _():
        st_ref[...] = jnp.zeros(st_ref.shape, F32)

    z = _dot(ad_ref[...], up_ref[...]) + ab_ref[...]
    la = -_softplus(-z) * (1.0 / GLA_TAU)
    bcum = _dot_r2(_tri_incl(C), la)
    row = _iota((SUB, C), 0)
    col = _iota((SUB, C), 1)
    krow = _iota((C, 1), 0)
    scale = GLA_DK ** -0.5
    for h in range(GLA_HEADS):
        ks = slice(h * GLA_DK_PAD, (h + 1) * GLA_DK_PAD)
        vs = slice(h * GLA_DV_PAD, (h + 1) * GLA_DV_PAD)
        q = q_ref[:, ks] * scale
        k = k_ref[:, ks]
        v = v_ref[:, vs]
        b = bcum[:, ks]
        st = st_ref[h]
        att_rows = []
        for s in range(C // SUB):
            r0 = s * SUB
            ref_row = b[r0 - 1:r0, :] if s > 0 else jnp.zeros((1, GLA_DK_PAD), F32)
            qs = q[r0:r0 + SUB, :] * jnp.exp(b[r0:r0 + SUB, :] - ref_row)
            kexp = jnp.where(krow < r0 + SUB, ref_row - b, 0.0)
            kk = k * jnp.exp(kexp)
            att = _dot(qs, kk, NT)
            att_rows.append(jnp.where(col <= row + r0, att, 0.0))
        att = jnp.concatenate(att_rows, axis=0)
        out = _dot(q * jnp.exp(b), st, NT) + _dot(att, v)
        b_last = b[C - 1:C, :]
        kd = k * jnp.exp(b_last - b)
        v_sq = jnp.concatenate([v, jnp.zeros((GLA_DV_PAD - C, GLA_DV_PAD), F32)], axis=0)
        kd_sq = jnp.concatenate([kd, jnp.zeros((GLA_DV_PAD - C, GLA_DK_PAD), F32)], axis=0)
        st_ref[h] = st * jnp.exp(b_last) + _dot(v_sq.T, kd_sq)
        ms = jnp.sum(out * out, axis=-1, keepdims=True) * (1.0 / GLA_DV)
        o = out * lax.rsqrt(ms + NORM_EPS) * ng_ref[:, vs]
        gg = g_ref[:, vs]
        o_ref[:, vs] = (o * (gg * _sigmoid(gg))).astype(o_ref.dtype)


def _gla(proj, up_p, ab_p, ng_p, B, S):
    T = B * S
    C = GLA_CHUNK
    nc = S // C
    return pl.pallas_call(
        _gla_kernel,
        grid=(B, nc),
        in_specs=[pl.BlockSpec((C, GLA_QK_PAD), lambda b, i: (b * nc + i, COL_Q // GLA_QK_PAD)),
                  pl.BlockSpec((C, GLA_QK_PAD), lambda b, i: (b * nc + i, COL_K // GLA_QK_PAD)),
                  pl.BlockSpec((C, GLA_V_PAD), lambda b, i: (b * nc + i, COL_V // GLA_V_PAD)),
                  pl.BlockSpec((C, GLA_V_PAD), lambda b, i: (b * nc + i, COL_G // GLA_V_PAD)),
                  pl.BlockSpec((C, LANES), lambda b, i: (b * nc + i, COL_AD // LANES)),
                  pl.BlockSpec((LANES, GLA_QK_PAD), lambda b, i: (0, 0)),
                  pl.BlockSpec((1, GLA_QK_PAD), lambda b, i: (0, 0)),
                  pl.BlockSpec((1, GLA_V_PAD), lambda b, i: (0, 0))],
        out_specs=pl.BlockSpec((C, GLA_V_PAD), lambda b, i: (b * nc + i, 0)),
        out_shape=jax.ShapeDtypeStruct((T, GLA_V_PAD), BF16),
        scratch_shapes=[pltpu.VMEM((GLA_HEADS, GLA_DV_PAD, GLA_DK_PAD), F32)],
        compiler_params=_cp(("arbitrary", "arbitrary")),
        name="gla_chunked",
    )(proj, proj, proj, proj, proj, up_p, ab_p, ng_p)


def _rwkv_kernel(y_ref, mu_ref, w0_ref, w2_ref, a0_ref, a2_ref, g2_ref, kk_ref, ka_ref, rk_ref,
                 lg_ref, lb_ref, o_ref, st_ref, prev_ref):
    C = RWKV_CHUNK
    W = RWKV_WIDTH

    @pl.when(pl.program_id(1) == 0)
    def _():
        st_ref[...] = jnp.zeros(st_ref.shape, F32)
        prev_ref[...] = jnp.zeros(prev_ref.shape, F32)

    y = y_ref[...]
    shifted = jnp.where(_iota((C, 1), 0) == 0, prev_ref[...], pltpu.roll(y, 1, 0))
    prev_ref[...] = y[C - 1:C, :]
    y = y + (shifted - y) * mu_ref[...]
    r_all = y[:, 0:W]
    k_all = y[:, W:2 * W]
    v_all = y[:, 2 * W:3 * W]
    wdad = y[:, 3 * W:3 * W + LANES]
    gd = y[:, 3 * W + LANES:3 * W + 2 * LANES]
    w_all = -_softplus(-(w0_ref[...] + _dot(jnp.tanh(wdad), w2_ref[...]))) - 0.5
    ld_all = -jnp.exp(w_all)
    a_all = _sigmoid(a0_ref[...] + _dot(wdad, a2_ref[...]))
    gate_all = _dot(_sigmoid(gd), g2_ref[...])

    lane = _iota((1, LANES), 1)
    m0 = (lane < RWKV_HEAD_DIM).astype(F32)
    m1 = 1.0 - m0
    seg = (_iota((LANES, LANES), 0) < RWKV_HEAD_DIM) == (_iota((LANES, LANES), 1) < RWKV_HEAD_DIM)
    seg_ones = seg.astype(BF16)
    ri = _iota((2 * C, 2 * C), 0)
    ci = _iota((2 * C, 2 * C), 1)
    same = (ri < C) == (ci < C)
    strict = same & (ci < ri)
    incl = same & (ci <= ri)
    eye = (ri == ci).astype(F32)
    tri = _tri_incl(C)

    def stack(x):
        return jnp.concatenate([x * m0, x * m1], axis=0)

    for p in range(RWKV_PAIRS):
        sl = slice(p * LANES, (p + 1) * LANES)
        r, k, v, ld, a = r_all[:, sl], k_all[:, sl], v_all[:, sl], ld_all[:, sl], a_all[:, sl]
        kk = k * kk_ref[:, sl]
        nrm = jnp.sqrt(_dot_l2(kk * kk, seg_ones))
        kk = kk / jnp.maximum(nrm, 1e-12)
        k = k * (1.0 + (a - 1.0) * ka_ref[:, sl])
        alpha = -kk
        beta = kk * a

        cum = _dot_r2(tri, ld)
        c_last = cum[C - 1:C, :]
        at = alpha * jnp.exp(cum - ld)
        rt = r * jnp.exp(cum)
        pinv = jnp.exp(-cum)
        bt = beta * pinv
        kt = k * pinv
        dlast = jnp.exp(c_last - cum)
        at2, rt2 = stack(at), stack(rt)
        bt2 = jnp.concatenate([bt, bt], axis=0)
        kt2 = jnp.concatenate([kt, kt], axis=0)
        v2 = stack(v)
        g = st_ref[p]

        a_ab = jnp.where(strict, _dot(at2, bt2, NT), 0.0)
        a_ak = jnp.where(strict, _dot(at2, kt2, NT), 0.0)
        l_rb = jnp.where(incl, _dot(rt2, bt2, NT), 0.0)
        l_rk = jnp.where(incl, _dot(rt2, kt2, NT), 0.0)
        rhs = _dot(at2, g, NT) + _dot(a_ak, v2)
        t = eye + a_ab
        m = a_ab
        for _ in range(int(np.log2(C)) - 1):
            m = _dot(m, m)
            t = t + _dot(t, m)
        u2 = _dot(t, rhs)
        o2 = _dot(rt2, g, NT) + _dot(l_rb, u2) + _dot(l_rk, v2)
        o = o2[0:C, :] + o2[C:2 * C, :]
        upd = _dot(u2.T, stack(beta * dlast)) + _dot(v2.T, stack(k * dlast))
        st_ref[p] = g * jnp.exp(c_last) + jnp.where(seg, upd, 0.0)

        mean = _dot_l2(o, seg_ones) * (1.0 / RWKV_HEAD_DIM)
        oc = o - mean
        var = _dot_l2(oc * oc, seg_ones) * (1.0 / RWKV_HEAD_DIM)
        on = oc * lax.rsqrt(var + RWKV_LNX_EPS) * lg_ref[:, sl] + lb_ref[:, sl]
        bonus = _dot_l2(r * k * rk_ref[:, sl], seg_ones) * v
        o_ref[:, sl] = ((on + bonus) * gate_all[:, sl]).astype(o_ref.dtype)


def _rwkv(proj, mu, w0, w2p, a0, a2p, g2, kk, ka, rk, lg, lb, B, S):
    T = B * S
    C = RWKV_CHUNK
    nc = S // C
    W = RWKV_WIDTH
    row = lambda n: pl.BlockSpec((1, n), lambda b, i: (0, 0))
    mat = lambda: pl.BlockSpec((LANES, W), lambda b, i: (0, 0))
    return pl.pallas_call(
        _rwkv_kernel,
        grid=(B, nc),
        in_specs=[pl.BlockSpec((C, RWKV_IN), lambda b, i: (b * nc + i, COL_RWKV // RWKV_IN)),
                  row(RWKV_IN), row(W), mat(), row(W), mat(), mat(), row(W), row(W), row(W), row(W), row(W)],
        out_specs=pl.BlockSpec((C, W), lambda b, i: (b * nc + i, 0)),
        out_shape=jax.ShapeDtypeStruct((T, W), BF16),
        scratch_shapes=[pltpu.VMEM((RWKV_PAIRS, LANES, LANES), F32),
                        pltpu.VMEM((1, RWKV_IN), F32)],
        compiler_params=_cp(("arbitrary", "arbitrary")),
        name="rwkv7_chunked",
    )(proj, mu, w0, w2p, a0, a2p, g2, kk, ka, rk, lg, lb)


def _take_top(work, n):
    rows = _iota(work.shape, 0)
    vals = []
    for _ in range(n):
        m = jnp.max(work, axis=0, keepdims=True)
        first = jnp.min(jnp.where(work == m, rows, work.shape[0]), axis=0, keepdims=True)
        work = jnp.where(rows == first, -jnp.inf, work)
        vals.append(m)
    return vals


def _peer_score_kernel(q_ref, k1_ref, k2_ref, th_ref, c1_ref, s2_ref, e2_ref):
    K = PEER_TOPK
    for h in range(PEER_HEADS):
        base = h * 2 * PEER_HALF
        s1 = _dot(k1_ref[h], q_ref[:, base:base + PEER_HALF], NT)
        s2 = _dot(k2_ref[h], q_ref[:, base + PEER_HALF:base + 2 * PEER_HALF], NT)
        v1 = _take_top(s1, K + 1)
        v2 = _take_top(s2, K + 1)
        cand = [v1[a] + v2[b] for a in range(K + 1) for b in range(K + 1) if (a + 1) * (b + 1) <= K + 1]
        cand += [jnp.full_like(cand[0], -jnp.inf)] * (-len(cand) % 8)
        best = _take_top(jnp.concatenate(cand, axis=0), K + 1)
        top = best[0]
        z = best[0] - top
        z = jnp.exp(z)
        for c in best[1:K]:
            z = z + jnp.exp(c - top)
        thr = 0.5 * (best[K - 1] + best[K])
        th_ref[h] = thr - s1
        c1_ref[h] = jnp.exp(s1 - v1[0]) / z
        s2_ref[h] = s2
        e2_ref[h] = jnp.exp(s2 - v2[0])


def _peer_scores(q, k1, k2, tt):
    T = q.shape[0]
    H, NK = PEER_HEADS, PEER_NKEYS
    out = jax.ShapeDtypeStruct((H, NK, T), F32)
    ospec = pl.BlockSpec((H, NK, tt), lambda i: (0, 0, i))
    return pl.pallas_call(
        _peer_score_kernel,
        grid=(T // tt,),
        in_specs=[pl.BlockSpec((tt, q.shape[1]), lambda i: (i, 0)),
                  pl.BlockSpec(k1.shape, lambda i: (0, 0, 0)),
                  pl.BlockSpec(k2.shape, lambda i: (0, 0, 0))],
        out_specs=[ospec, ospec, ospec, ospec],
        out_shape=[out, out, out, out],
        compiler_params=_cp(("parallel",)),
        name="peer_scores",
    )(q, k1, k2)


def _gelu(x):
    return 0.5 * x * (1.0 + lax.erf(x * (2.0 ** -0.5)))


def _peer_dense_kernel(h_ref, u_ref, vt_ref, th_ref, c1_ref, s2_ref, e2_ref,
                       o_ref, ht_ref, acc_ref, *, n_i1):
    j = pl.program_id(1)

    @pl.when(j == 0)
    def _():
        ht_ref[...] = h_ref[...].astype(F32).T.astype(BF16)
        acc_ref[...] = jnp.zeros(acc_ref.shape, F32)

    act = jnp.dot(u_ref[...], ht_ref[...], preferred_element_type=F32)
    blocks = []
    for il in range(n_i1):
        w = None
        for h in range(PEER_HEADS):
            th = th_ref[h, il:il + 1, :]
            c1 = c1_ref[h, il:il + 1, :]
            term = jnp.where(s2_ref[h] >= th, e2_ref[h] * c1, 0.0)
            w = term if w is None else w + term
        blocks.append(w)
    wt = jnp.concatenate(blocks, axis=0)
    coeff = (wt * _gelu(act)).astype(BF16)
    acc_ref[...] += jnp.dot(vt_ref[...], coeff, preferred_element_type=F32)

    @pl.when(j == pl.num_programs(1) - 1)
    def _():
        o_ref[...] = acc_ref[...].T


def _peer_dense(h, u, vt, th, c1, s2, e2, tm, te):
    T, D = h.shape
    H, NK = PEER_HEADS, PEER_NKEYS
    n_i1 = te // NK
    row_spec = pl.BlockSpec((H, n_i1, tm), lambda i, j: (0, j, i))
    all_spec = pl.BlockSpec((H, NK, tm), lambda i, j: (0, 0, i))
    return pl.pallas_call(
        functools.partial(_peer_dense_kernel, n_i1=n_i1),
        grid=(T // tm, PEER_N // te),
        in_specs=[pl.BlockSpec((tm, D), lambda i, j: (i, 0)),
                  pl.BlockSpec((te, D), lambda i, j: (j, 0)),
                  pl.BlockSpec((D, te), lambda i, j: (0, j)),
                  row_spec, row_spec, all_spec, all_spec],
        out_specs=pl.BlockSpec((tm, D), lambda i, j: (i, 0)),
        out_shape=jax.ShapeDtypeStruct((T, D), F32),
        scratch_shapes=[pltpu.VMEM((D, tm), BF16), pltpu.VMEM((D, tm), F32)],
        compiler_params=_cp(("parallel", "arbitrary")),
        name="peer_dense",
    )(h, u, vt, th, c1, s2, e2)


def _pad_heads(w, heads, dim, dim_pad):
    lead = w.shape[:-1]
    w = w.reshape(lead + (heads, dim))
    w = jnp.pad(w, [(0, 0)] * len(lead) + [(0, 0), (0, dim_pad - dim)])
    return w.reshape(lead + (heads * dim_pad,))


def _layout_w_in(w_in):
    pool = w_in[..., 0:512]
    q = _pad_heads(w_in[..., 512:896], GLA_HEADS, GLA_DK, GLA_DK_PAD)
    k = _pad_heads(w_in[..., 896:1280], GLA_HEADS, GLA_DK, GLA_DK_PAD)
    v = _pad_heads(w_in[..., 1280:2048], GLA_HEADS, GLA_DV, GLA_DV_PAD)
    g = _pad_heads(w_in[..., 2048:2816], GLA_HEADS, GLA_DV, GLA_DV_PAD)
    ad = jnp.pad(w_in[..., 2816:2832], [(0, 0)] * (w_in.ndim - 1) + [(0, LANES - GLA_LORA)])
    rw = w_in[..., 2832:5392]
    return jnp.concatenate([rw, pool, q, k, v, g, ad], axis=-1).astype(BF16)


def kernel(x, c, ada_w, ada_b, norm1_g, w_in, pool_w, pool_scale, gla_alpha_up, gla_alpha_b, gla_norm_g, rwkv_mu, rwkv_w0, rwkv_w2, rwkv_a0, rwkv_a2, rwkv_g2, rwkv_kk, rwkv_ka, rwkv_rk, rwkv_lnx_g, rwkv_lnx_b, w_out, norm2_g, peer_wq, peer_k1, peer_k2, peer_u, peer_v, final_g):
    B, S, D = x.shape
    L = ada_w.shape[0]
    T = B * S
    tm = min(512, S)
    tpb = S // tm
    W = RWKV_WIDTH

    mod = _modulation(c, ada_w, ada_b)
    x2 = x.reshape(T, D)
    ffn = None
    for l in range(L):
        mod3 = mod[l].reshape(B, 1, 6 * D)
        if l == 0:
            h = _normmod(x2, norm1_g[l], mod3, 0, 1, tpb, tm)
        else:
            x2, h = _resid_normmod(x2, ffn, mod[l - 1].reshape(B, 1, 6 * D), 5, norm1_g[l], mod3, 0, 1, tpb, tm)
        proj = _matmul(h, _layout_w_in(w_in[l]), tm, 896)
        y_pool = _pool(proj, pool_w[l], pool_scale[l], B, S, tm)
        up_p = jnp.pad(_pad_heads(gla_alpha_up[l], GLA_HEADS, GLA_DK, GLA_DK_PAD),
                       ((0, LANES - GLA_LORA), (0, 0)))
        ab_p = _pad_heads(gla_alpha_b[l], GLA_HEADS, GLA_DK, GLA_DK_PAD).reshape(1, -1)
        ng_p = _pad_heads(gla_norm_g[l], GLA_HEADS, GLA_DV, GLA_DV_PAD).reshape(1, -1)
        y_gla = _gla(proj, up_p, ab_p, ng_p, B, S)
        zeros = jnp.zeros((RWKV_HEAD_DIM, W), F32)
        w2p = jnp.concatenate([rwkv_w2[l], zeros], axis=0).astype(BF16)
        a2p = jnp.concatenate([zeros, rwkv_a2[l]], axis=0).astype(BF16)
        y_rwkv = _rwkv(proj, rwkv_mu[l].reshape(1, -1), rwkv_w0[l].reshape(1, W), w2p,
                       rwkv_a0[l].reshape(1, W), a2p, rwkv_g2[l].astype(BF16),
                       rwkv_kk[l].reshape(1, W), rwkv_ka[l].reshape(1, W), rwkv_rk[l].reshape(1, W),
                       rwkv_lnx_g[l].reshape(1, W), rwkv_lnx_b[l].reshape(1, W), B, S)
        wo = w_out[l]
        wo_g = jnp.pad(wo[512:1280].reshape(GLA_HEADS, GLA_DV, D),
                       ((0, 0), (0, GLA_DV_PAD - GLA_DV), (0, 0))).reshape(GLA_V_PAD, D)
        x2 = _outproj(y_pool, y_gla, y_rwkv, wo[0:512].astype(BF16), wo_g.astype(BF16),
                      wo[1280:].astype(BF16), x2, mod3, 2 * D // 512, tpb, tm, 512)
        h = _normmod(x2, norm2_g[l], mod3, 3, 4, tpb, tm)
        q = _matmul(h, peer_wq[l].astype(BF16), tm, 512)
        th, c1, s2, e2 = _peer_scores(q, peer_k1[l], peer_k2[l], tm)
        ffn = _peer_dense(h, peer_u[l].astype(BF16), peer_v[l].T.astype(BF16), th, c1, s2, e2, tm, 1024)
    out = _resid_final(x2, ffn, mod[L - 1].reshape(B, 1, 6 * D), 5, final_g, tpb, tm)
    return out.reshape(B, S, D)
```

```python
import functools

import jax
import jax.numpy as jnp
import numpy as np
from jax import lax
from jax.experimental import pallas as pl
from jax.experimental.pallas import tpu as pltpu

F32 = jnp.float32
BF16 = jnp.bfloat16

LANES = 128
NORM_EPS = 1e-6

POOL_WINDOWS = (2, 4, 8, 16)
POOL_GROUP_DIM = 128
POOL_WIDTH = 512
POOL_HALO = 16

GLA_HEADS = 4
GLA_DK = 96
GLA_DV = 192
GLA_DK_PAD = 128
GLA_DV_PAD = 256
GLA_LORA = 16
GLA_TAU = 16.0
GLA_CHUNK = 64
GLA_SUB = 16
GLA_QK_PAD = GLA_HEADS * GLA_DK_PAD
GLA_V_PAD = GLA_HEADS * GLA_DV_PAD

RWKV_WIDTH = 768
RWKV_HEAD_DIM = 64
RWKV_PAIRS = RWKV_WIDTH // LANES
RWKV_IN = 2560
RWKV_CHUNK = 64
RWKV_LNX_EPS = 64e-5

COL_RWKV = 0
COL_POOL = 2560
COL_Q = 3072
COL_K = 3584
COL_V = 4096
COL_G = 5120
COL_AD = 6144
IN_PAD = 6272

PEER_HEADS = 8
PEER_NKEYS = 128
PEER_HALF = 128
PEER_TOPK = 16
PEER_N = PEER_NKEYS * PEER_NKEYS
PEER_DENSE_GROUPS = 4

NN = (((1,), (0,)), ((), ()))
NT = (((1,), (1,)), ((), ()))

VMEM_LIMIT = 56 * 1024 * 1024


def _cp(sem):
    return pltpu.CompilerParams(dimension_semantics=sem, vmem_limit_bytes=VMEM_LIMIT)


def _dot(a, b, dims=NN):
    return lax.dot_general(a.astype(BF16), b.astype(BF16), dims, preferred_element_type=F32)


def _split(x):
    hi = x.astype(BF16)
    lo = (x - hi.astype(F32)).astype(BF16)
    return hi, lo


def _dot_l2(a, b, dims=NN):
    hi, lo = _split(a)
    bb = b.astype(BF16)
    return (lax.dot_general(hi, bb, dims, preferred_element_type=F32)
            + lax.dot_general(lo, bb, dims, preferred_element_type=F32))


def _dot_r2(a, b):
    hi, lo = _split(b)
    ab = a.astype(BF16)
    return (lax.dot_general(ab, hi, NN, preferred_element_type=F32)
            + lax.dot_general(ab, lo, NN, preferred_element_type=F32))


def _iota(shape, dim):
    return lax.broadcasted_iota(jnp.int32, shape, dim)


def _tri_incl(n):
    return (_iota((n, n), 0) >= _iota((n, n), 1)).astype(BF16)


def _sigmoid(x):
    return 1.0 / (1.0 + jnp.exp(-x))


def _softplus(x):
    return jnp.maximum(x, 0.0) + jnp.log1p(jnp.exp(-jnp.abs(x)))


def _mod_kernel(c_ref, w_ref, b_ref, o_ref):
    c = c_ref[...]
    ca = c * _sigmoid(c)
    o_ref[...] = jnp.dot(ca, w_ref[...], precision=lax.Precision.HIGHEST,
                         preferred_element_type=F32) + b_ref[...]


def _modulation(c, ada_w, ada_b):
    L, D, N6 = ada_w.shape
    B = c.shape[0]
    tn = 1024
    return pl.pallas_call(
        _mod_kernel,
        grid=(L, N6 // tn),
        in_specs=[pl.BlockSpec((B, D), lambda l, j: (0, 0)),
                  pl.BlockSpec((None, D, tn), lambda l, j: (l, 0, j)),
                  pl.BlockSpec((None, 1, tn), lambda l, j: (l, 0, j))],
        out_specs=pl.BlockSpec((None, B, tn), lambda l, j: (l, 0, j)),
        out_shape=jax.ShapeDtypeStruct((L, B, N6), F32),
        compiler_params=_cp(("parallel", "parallel")),
        name="adaln_mod",
    )(c, ada_w, ada_b.reshape(L, 1, N6))


def _normmod_kernel(x_ref, g_ref, sh_ref, sc_ref, o_ref):
    x = x_ref[...]
    y = x * lax.rsqrt(jnp.mean(x * x, axis=-1, keepdims=True) + NORM_EPS) * g_ref[...]
    o_ref[...] = (y * (1.0 + sc_ref[...]) + sh_ref[...]).astype(o_ref.dtype)


def _normmod(x2, g, mod3, sh_blk, sc_blk, tiles_per_batch, tm):
    T, D = x2.shape
    return pl.pallas_call(
        _normmod_kernel,
        grid=(T // tm,),
        in_specs=[pl.BlockSpec((tm, D), lambda i: (i, 0)),
                  pl.BlockSpec((1, D), lambda i: (0, 0)),
                  pl.BlockSpec((None, 1, D), lambda i: (i // tiles_per_batch, 0, sh_blk)),
                  pl.BlockSpec((None, 1, D), lambda i: (i // tiles_per_batch, 0, sc_blk))],
        out_specs=pl.BlockSpec((tm, D), lambda i: (i, 0)),
        out_shape=jax.ShapeDtypeStruct((T, D), BF16),
        compiler_params=_cp(("parallel",)),
        name="rmsnorm_adaln",
    )(x2, g.reshape(1, D), mod3, mod3)


def _resid_normmod_kernel(x_ref, f_ref, gate_ref, g_ref, sh_ref, sc_ref, xo_ref, h_ref):
    x = x_ref[...] + gate_ref[...] * f_ref[...]
    xo_ref[...] = x
    y = x * lax.rsqrt(jnp.mean(x * x, axis=-1, keepdims=True) + NORM_EPS) * g_ref[...]
    h_ref[...] = (y * (1.0 + sc_ref[...]) + sh_ref[...]).astype(h_ref.dtype)


def _resid_normmod(x2, f, mod_prev3, gate_blk, g, mod3, sh_blk, sc_blk, tiles_per_batch, tm):
    T, D = x2.shape
    row = lambda blk: pl.BlockSpec((None, 1, D), lambda i: (i // tiles_per_batch, 0, blk))
    tile = pl.BlockSpec((tm, D), lambda i: (i, 0))
    return pl.pallas_call(
        _resid_normmod_kernel,
        grid=(T // tm,),
        in_specs=[tile, tile, row(gate_blk), pl.BlockSpec((1, D), lambda i: (0, 0)), row(sh_blk), row(sc_blk)],
        out_specs=[tile, tile],
        out_shape=[jax.ShapeDtypeStruct((T, D), F32), jax.ShapeDtypeStruct((T, D), BF16)],
        compiler_params=_cp(("parallel",)),
        name="residual_rmsnorm_adaln",
    )(x2, f, mod_prev3, g.reshape(1, D), mod3, mod3)


def _resid_final_kernel(x_ref, f_ref, gate_ref, g_ref, o_ref):
    x = x_ref[...] + gate_ref[...] * f_ref[...]
    o_ref[...] = x * lax.rsqrt(jnp.mean(x * x, axis=-1, keepdims=True) + NORM_EPS) * g_ref[...]


def _resid_final(x2, f, mod_prev3, gate_blk, g, tiles_per_batch, tm):
    T, D = x2.shape
    tile = pl.BlockSpec((tm, D), lambda i: (i, 0))
    return pl.pallas_call(
        _resid_final_kernel,
        grid=(T // tm,),
        in_specs=[tile, tile,
                  pl.BlockSpec((None, 1, D), lambda i: (i // tiles_per_batch, 0, gate_blk)),
                  pl.BlockSpec((1, D), lambda i: (0, 0))],
        out_specs=tile,
        out_shape=jax.ShapeDtypeStruct((T, D), F32),
        compiler_params=_cp(("parallel",)),
        name="residual_final_rmsnorm",
    )(x2, f, mod_prev3, g.reshape(1, D))


def _matmul_kernel(a_ref, w_ref, o_ref):
    o_ref[...] = jnp.dot(a_ref[...], w_ref[...], preferred_element_type=F32)


def _matmul(a, w, tm, tn):
    M, K = a.shape
    N = w.shape[1]
    return pl.pallas_call(
        _matmul_kernel,
        grid=(M // tm, N // tn),
        in_specs=[pl.BlockSpec((tm, K), lambda i, j: (i, 0)),
                  pl.BlockSpec((K, tn), lambda i, j: (0, j))],
        out_specs=pl.BlockSpec((tm, tn), lambda i, j: (i, j)),
        out_shape=jax.ShapeDtypeStruct((M, N), F32),
        compiler_params=_cp(("parallel", "parallel")),
        name="projection",
    )(a, w)


def _outproj_kernel(yp_ref, yg_ref, yr_ref, wp_ref, wg_ref, wr_ref, x_ref, gate_ref, o_ref):
    acc = jnp.dot(yp_ref[...], wp_ref[...], preferred_element_type=F32)
    acc += jnp.dot(yg_ref[...], wg_ref[...], preferred_element_type=F32)
    acc += jnp.dot(yr_ref[...], wr_ref[...], preferred_element_type=F32)
    o_ref[...] = x_ref[...] + gate_ref[...] * acc


def _outproj(yp, yg, yr, wp, wg, wr, x2, mod3, gate_blk0, tiles_per_batch, tm, tn):
    T, D = x2.shape
    return pl.pallas_call(
        _outproj_kernel,
        grid=(T // tm, D // tn),
        in_specs=[pl.BlockSpec((tm, yp.shape[1]), lambda i, j: (i, 0)),
                  pl.BlockSpec((tm, yg.shape[1]), lambda i, j: (i, 0)),
                  pl.BlockSpec((tm, yr.shape[1]), lambda i, j: (i, 0)),
                  pl.BlockSpec((wp.shape[0], tn), lambda i, j: (0, j)),
                  pl.BlockSpec((wg.shape[0], tn), lambda i, j: (0, j)),
                  pl.BlockSpec((wr.shape[0], tn), lambda i, j: (0, j)),
                  pl.BlockSpec((tm, tn), lambda i, j: (i, j)),
                  pl.BlockSpec((None, 1, tn), lambda i, j: (i // tiles_per_batch, 0, gate_blk0 + j))],
        out_specs=pl.BlockSpec((tm, tn), lambda i, j: (i, j)),
        out_shape=jax.ShapeDtypeStruct((T, D), F32),
        compiler_params=_cp(("parallel", "parallel")),
        name="out_projection",
    )(yp, yg, yr, wp, wg, wr, x2, mod3)


def _pool_kernel(p_ref, w_ref, scale_ref, o_ref, ext_ref, *, ts):
    i = pl.program_id(1)

    @pl.when(i == 0)
    def _():
        ext_ref[0:POOL_HALO, :] = jnp.zeros((POOL_HALO, POOL_WIDTH), F32)

    p = p_ref[...]
    ext_ref[POOL_HALO:POOL_HALO + ts, :] = p
    pos = (i * ts + _iota((ts, 1), 0) + 1).astype(F32)
    for g, win in enumerate(POOL_WINDOWS):
        lo = g * POOL_GROUP_DIM
        acc = p[:, lo:lo + POOL_GROUP_DIM]
        for j in range(1, win):
            acc = acc + ext_ref[POOL_HALO - j:POOL_HALO - j + ts, lo:lo + POOL_GROUP_DIM]
        d = acc / jnp.minimum(pos, float(win)) - p[:, lo:lo + POOL_GROUP_DIM]
        y = _dot(d, w_ref[g]) * scale_ref[:, lo:lo + POOL_GROUP_DIM]
        o_ref[:, lo:lo + POOL_GROUP_DIM] = y.astype(o_ref.dtype)
    ext_ref[0:POOL_HALO, :] = ext_ref[ts:ts + POOL_HALO, :]


def _pool(proj, pool_w, pool_scale, B, S, ts):
    T = B * S
    nt = S // ts
    return pl.pallas_call(
        functools.partial(_pool_kernel, ts=ts),
        grid=(B, nt),
        in_specs=[pl.BlockSpec((ts, POOL_WIDTH), lambda b, i: (b * nt + i, COL_POOL // POOL_WIDTH)),
                  pl.BlockSpec(pool_w.shape, lambda b, i: (0, 0, 0)),
                  pl.BlockSpec((1, POOL_WIDTH), lambda b, i: (0, 0))],
        out_specs=pl.BlockSpec((ts, POOL_WIDTH), lambda b, i: (b * nt + i, 0)),
        out_shape=jax.ShapeDtypeStruct((T, POOL_WIDTH), BF16),
        scratch_shapes=[pltpu.VMEM((ts + POOL_HALO, POOL_WIDTH), F32)],
        compiler_params=_cp(("arbitrary", "arbitrary")),
        name="causal_pool",
    )(proj, pool_w, pool_scale.reshape(1, POOL_WIDTH))


def _gla_kernel(q_ref, k_ref, v_ref, g_ref, ad_ref, up_ref, ab_ref, ng_ref, o_ref, st_ref):
    C, SUB = GLA_CHUNK, GLA_SUB

    @pl.when(pl.program_id(1) == 0)
    def _():
        st_ref[...] = jnp.zeros(st_ref.shape, F32)

    z = _dot(ad_ref[...], up_ref[...]) + ab_ref[...]
    la = -_softplus(-z) * (1.0 / GLA_TAU)
    bcum = _dot_r2(_tri_incl(C), la)
    rowc = _iota((C, 1), 0)
    causal = _iota((C, C), 1) <= _iota((C, C), 0)
    scale = GLA_DK ** -0.5
    H = range(GLA_HEADS)
    subs = range(C // SUB)
    ks = [slice(h * GLA_DK_PAD, (h + 1) * GLA_DK_PAD) for h in H]
    vs = [slice(h * GLA_DV_PAD, (h + 1) * GLA_DV_PAD) for h in H]
    q = [q_ref[:, ks[h]] * scale for h in H]
    k = [k_ref[:, ks[h]] for h in H]
    v = [v_ref[:, vs[h]] for h in H]
    b = [bcum[:, ks[h]] for h in H]
    st = [st_ref[h] for h in H]
    refs = [[b[h][s * SUB - 1:s * SUB, :] if s else jnp.zeros((1, GLA_DK_PAD), F32) for s in subs] for h in H]
    in_sub = [(rowc >= s * SUB) & (rowc < (s + 1) * SUB) for s in subs]
    q_bd = [jnp.concatenate([jnp.where(in_sub[s], q[h] * jnp.exp(b[h] - refs[h][s]), 0.0) for s in subs], axis=1)
            for h in H]
    k_cat = [jnp.concatenate([k[h] * jnp.exp(jnp.where(rowc < (s + 1) * SUB, refs[h][s] - b[h], 0.0)) for s in subs],
                             axis=1) for h in H]
    att = [jnp.where(causal, _dot(q_bd[h], k_cat[h], NT), 0.0) for h in H]
    out = [_dot(q[h] * jnp.exp(b[h]), st[h], NT) + _dot(att[h], v[h]) for h in H]
    b_last = [x[C - 1:C, :] for x in b]
    zpad_v = jnp.zeros((GLA_DV_PAD - C, GLA_DV_PAD), F32)
    zpad_k = jnp.zeros((GLA_DV_PAD - C, GLA_DK_PAD), F32)
    upd = [_dot(jnp.concatenate([v[h], zpad_v], axis=0).T,
                jnp.concatenate([k[h] * jnp.exp(b_last[h] - b[h]), zpad_k], axis=0)) for h in H]
    for h in H:
        st_ref[h] = st[h] * jnp.exp(b_last[h]) + upd[h]
    for h in H:
        ms = jnp.sum(out[h] * out[h], axis=-1, keepdims=True) * (1.0 / GLA_DV)
        o = out[h] * lax.rsqrt(ms + NORM_EPS) * ng_ref[:, vs[h]]
        gg = g_ref[:, vs[h]]
        o_ref[:, vs[h]] = (o * (gg * _sigmoid(gg))).astype(o_ref.dtype)


def _gla(proj, up_p, ab_p, ng_p, B, S):
    T = B * S
    C = GLA_CHUNK
    nc = S // C
    return pl.pallas_call(
        _gla_kernel,
        grid=(B, nc),
        in_specs=[pl.BlockSpec((C, GLA_QK_PAD), lambda b, i: (b * nc + i, COL_Q // GLA_QK_PAD)),
                  pl.BlockSpec((C, GLA_QK_PAD), lambda b, i: (b * nc + i, COL_K // GLA_QK_PAD)),
                  pl.BlockSpec((C, GLA_V_PAD), lambda b, i: (b * nc + i, COL_V // GLA_V_PAD)),
                  pl.BlockSpec((C, GLA_V_PAD), lambda b, i: (b * nc + i, COL_G // GLA_V_PAD)),
                  pl.BlockSpec((C, LANES), lambda b, i: (b * nc + i, COL_AD // LANES)),
                  pl.BlockSpec((LANES, GLA_QK_PAD), lambda b, i: (0, 0)),
                  pl.BlockSpec((1, GLA_QK_PAD), lambda b, i: (0, 0)),
                  pl.BlockSpec((1, GLA_V_PAD), lambda b, i: (0, 0))],
        out_specs=pl.BlockSpec((C, GLA_V_PAD), lambda b, i: (b * nc + i, 0)),
        out_shape=jax.ShapeDtypeStruct((T, GLA_V_PAD), BF16),
        scratch_shapes=[pltpu.VMEM((GLA_HEADS, GLA_DV_PAD, GLA_DK_PAD), F32)],
        compiler_params=_cp(("arbitrary", "arbitrary")),
        name="gla_chunked",
    )(proj, proj, proj, proj, proj, up_p, ab_p, ng_p)


def _rwkv_kernel(y_ref, mu_ref, w0_ref, w2_ref, a0_ref, a2_ref, g2_ref, kk_ref, ka_ref, rk_ref,
                 lg_ref, lb_ref, o_ref, st_ref, prev_ref):
    C = RWKV_CHUNK
    W = RWKV_WIDTH

    @pl.when(pl.program_id(1) == 0)
    def _():
        st_ref[...] = jnp.zeros(st_ref.shape, F32)
        prev_ref[...] = jnp.zeros(prev_ref.shape, F32)

    y = y_ref[...]
    shifted = jnp.where(_iota((C, 1), 0) == 0, prev_ref[...], pltpu.roll(y, 1, 0))
    prev_ref[...] = y[C - 1:C, :]
    y = y + (shifted - y) * mu_ref[...]
    r_all = y[:, 0:W]
    k_all = y[:, W:2 * W]
    v_all = y[:, 2 * W:3 * W]
    wdad = y[:, 3 * W:3 * W + LANES]
    gd = y[:, 3 * W + LANES:3 * W + 2 * LANES]
    w_all = -_softplus(-(w0_ref[...] + _dot(jnp.tanh(wdad), w2_ref[...]))) - 0.5
    ld_all = -jnp.exp(w_all)
    a_all = _sigmoid(a0_ref[...] + _dot(wdad, a2_ref[...]))
    gate_all = _dot(_sigmoid(gd), g2_ref[...])

    lane = _iota((1, LANES), 1)
    m0 = (lane < RWKV_HEAD_DIM).astype(F32)
    m1 = 1.0 - m0
    seg = (_iota((LANES, LANES), 0) < RWKV_HEAD_DIM) == (_iota((LANES, LANES), 1) < RWKV_HEAD_DIM)
    seg_ones = seg.astype(BF16)
    ri = _iota((2 * C, 2 * C), 0)
    ci = _iota((2 * C, 2 * C), 1)
    same = (ri < C) == (ci < C)
    strict = same & (ci < ri)
    incl = same & (ci <= ri)
    eye = (ri == ci).astype(F32)
    tri = _tri_incl(C)

    def stack(x):
        return jnp.concatenate([x * m0, x * m1], axis=0)

    P = range(RWKV_PAIRS)
    lanes = [slice(p * LANES, (p + 1) * LANES) for p in P]
    r = [r_all[:, s] for s in lanes]
    v = [v_all[:, s] for s in lanes]
    a = [a_all[:, s] for s in lanes]
    kk = [k_all[:, s] * kk_ref[:, s] for s in lanes]
    nrm = [jnp.sqrt(_dot_l2(x * x, seg_ones)) for x in kk]
    kk = [kk[p] / jnp.maximum(nrm[p], 1e-12) for p in P]
    k = [k_all[:, lanes[p]] * (1.0 + (a[p] - 1.0) * ka_ref[:, lanes[p]]) for p in P]
    beta = [kk[p] * a[p] for p in P]

    cum_all = _dot_r2(tri, ld_all)
    cum = [cum_all[:, s] for s in lanes]
    c_last = [x[C - 1:C, :] for x in cum]
    at = [-kk[p] * jnp.exp(cum[p] - ld_all[:, lanes[p]]) for p in P]
    rt = [r[p] * jnp.exp(cum[p]) for p in P]
    pinv = [jnp.exp(-x) for x in cum]
    bt = [beta[p] * pinv[p] for p in P]
    kt = [k[p] * pinv[p] for p in P]
    dlast = [jnp.exp(c_last[p] - cum[p]) for p in P]
    ar2 = [jnp.concatenate([stack(at[p]), stack(rt[p])], axis=0) for p in P]
    bk2 = [jnp.concatenate([bt[p], bt[p], kt[p], kt[p]], axis=0) for p in P]
    v2 = [stack(x) for x in v]
    g = [st_ref[p] for p in P]

    sc = [_dot(ar2[p], bk2[p], NT) for p in P]
    a_ab = [jnp.where(strict, x[0:2 * C, 0:2 * C], 0.0) for x in sc]
    a_ak = [jnp.where(strict, x[0:2 * C, 2 * C:4 * C], 0.0) for x in sc]
    l_rbk = [jnp.concatenate([jnp.where(incl, x[2 * C:4 * C, 0:2 * C], 0.0),
                              jnp.where(incl, x[2 * C:4 * C, 2 * C:4 * C], 0.0)], axis=1) for x in sc]
    x0 = [_dot(ar2[p], g[p], NT) for p in P]
    rhs = [x0[p][0:2 * C, :] + _dot(a_ak[p], v2[p]) for p in P]
    t = [eye + x for x in a_ab]
    m = [_dot(x, x) for x in a_ab]
    for _ in range(int(np.log2(C)) - 2):
        tm = [_dot(jnp.concatenate([t[p], m[p]], axis=0), m[p]) for p in P]
        t = [t[p] + tm[p][0:2 * C, :] for p in P]
        m = [x[2 * C:4 * C, :] for x in tm]
    t = [t[p] + _dot(t[p], m[p]) for p in P]
    u2 = [_dot(t[p], rhs[p]) for p in P]
    uv = [jnp.concatenate([u2[p], v2[p]], axis=0) for p in P]
    o2 = [x0[p][2 * C:4 * C, :] + _dot(l_rbk[p], uv[p]) for p in P]
    o = [x[0:C, :] + x[C:2 * C, :] for x in o2]
    bkd = [jnp.concatenate([stack(beta[p] * dlast[p]), stack(k[p] * dlast[p])], axis=0) for p in P]
    upd = [_dot(jnp.concatenate([u2[p].T, v2[p].T], axis=1), bkd[p]) for p in P]
    for p in P:
        st_ref[p] = g[p] * jnp.exp(c_last[p]) + jnp.where(seg, upd[p], 0.0)

    mean = [_dot_l2(x, seg_ones) * (1.0 / RWKV_HEAD_DIM) for x in o]
    oc = [o[p] - mean[p] for p in P]
    var = [_dot_l2(x * x, seg_ones) * (1.0 / RWKV_HEAD_DIM) for x in oc]
    bonus = [_dot_l2(r[p] * k[p] * rk_ref[:, lanes[p]], seg_ones) * v[p] for p in P]
    for p in P:
        s = lanes[p]
        on = oc[p] * lax.rsqrt(var[p] + RWKV_LNX_EPS) * lg_ref[:, s] + lb_ref[:, s]
        o_ref[:, s] = ((on + bonus[p]) * gate_all[:, s]).astype(o_ref.dtype)


def _rwkv(proj, mu, w0, w2p, a0, a2p, g2, kk, ka, rk, lg, lb, B, S):
    T = B * S
    C = RWKV_CHUNK
    nc = S // C
    W = RWKV_WIDTH
    row = lambda n: pl.BlockSpec((1, n), lambda b, i: (0, 0))
    mat = lambda: pl.BlockSpec((LANES, W), lambda b, i: (0, 0))
    return pl.pallas_call(
        _rwkv_kernel,
        grid=(B, nc),
        in_specs=[pl.BlockSpec((C, RWKV_IN), lambda b, i: (b * nc + i, COL_RWKV // RWKV_IN)),
                  row(RWKV_IN), row(W), mat(), row(W), mat(), mat(), row(W), row(W), row(W), row(W), row(W)],
        out_specs=pl.BlockSpec((C, W), lambda b, i: (b * nc + i, 0)),
        out_shape=jax.ShapeDtypeStruct((T, W), BF16),
        scratch_shapes=[pltpu.VMEM((RWKV_PAIRS, LANES, LANES), F32),
                        pltpu.VMEM((1, RWKV_IN), F32)],
        compiler_params=_cp(("arbitrary", "arbitrary")),
        name="rwkv7_chunked",
    )(proj, mu, w0, w2p, a0, a2p, g2, kk, ka, rk, lg, lb)


def _take_top(works, n):
    works = list(works)
    rows = _iota(works[0].shape, 0)
    n_rows = works[0].shape[0]
    ranks = [jnp.full(w.shape, float(n), F32) for w in works]
    vals = [[] for _ in works]
    for i in range(n):
        m = [jnp.max(w, axis=0, keepdims=True) for w in works]
        first = [jnp.min(jnp.where(w == mm, rows, n_rows), axis=0, keepdims=True) for w, mm in zip(works, m)]
        hit = [rows == f for f in first]
        works = [jnp.where(h, -jnp.inf, w) for h, w in zip(hit, works)]
        ranks = [jnp.where(h, float(i), r) for h, r in zip(hit, ranks)]
        for v, mm in zip(vals, m):
            v.append(mm)
    return vals, ranks


def _peer_score_kernel(q_ref, k1_ref, k2_ref, n1_ref, c1_ref, r2_ref, e2_ref):
    K = PEER_TOPK
    H = range(PEER_HEADS)
    pairs = [(a, b) for a in range(K) for b in range(K) if (a + 1) * (b + 1) <= K]
    n_pad = -len(pairs) % 8
    n_cand = len(pairs) + n_pad
    a_idx = _iota((K, n_cand), 0)
    c_idx = _iota((K, n_cand), 1)
    owner = jnp.zeros((K, n_cand), F32)
    for a in range(K):
        first = pairs.index((a, 0))
        owner = jnp.where((a_idx == a) & (c_idx >= first) & (c_idx < first + K // (a + 1)), 1.0, owner)
    v1, rank1, cand = [], [], []
    for h in H:
        base = h * 2 * PEER_HALF
        s1 = _dot(k1_ref[h], q_ref[:, base:base + PEER_HALF], NT)
        s2 = _dot(k2_ref[h], q_ref[:, base + PEER_HALF:base + 2 * PEER_HALF], NT)
        (va, vb), (ra, rb) = _take_top([s1, s2], K)
        neg = jnp.full_like(va[0], -jnp.inf)
        cand.append(jnp.concatenate([va[a] + vb[b] for a, b in pairs] + [neg] * n_pad, axis=0))
        v1.append(va[0])
        rank1.append(ra)
        c1_ref[h] = s1
        r2_ref[h] = rb.astype(r2_ref.dtype)
        e2_ref[h] = jnp.exp(s2 - vb[0]).astype(e2_ref.dtype)
    best, crank = _take_top(cand, K)
    for h in H:
        z = jnp.ones_like(best[h][0])
        for c in best[h][1:]:
            z = z + jnp.exp(c - best[h][0])
        n_sel = _dot(owner, (crank[h] < K).astype(F32))
        n1 = jnp.zeros_like(rank1[h])
        for a in range(K):
            n1 = jnp.where(rank1[h] == a, n_sel[a:a + 1, :], n1)
        n1_ref[h] = n1
        c1_ref[h] = jnp.exp(c1_ref[h] - v1[h]) / z


def _peer_scores(q, k1, k2, tt):
    T = q.shape[0]
    H, NK = PEER_HEADS, PEER_NKEYS
    out = jax.ShapeDtypeStruct((H, NK, T), F32)
    ospec = pl.BlockSpec((H, NK, tt), lambda i: (0, 0, i))
    return pl.pallas_call(
        _peer_score_kernel,
        grid=(T // tt,),
        in_specs=[pl.BlockSpec((tt, q.shape[1]), lambda i: (i, 0)),
                  pl.BlockSpec(k1.shape, lambda i: (0, 0, 0)),
                  pl.BlockSpec(k2.shape, lambda i: (0, 0, 0))],
        out_specs=[ospec, ospec, ospec, ospec],
        out_shape=[out, out, out, out],
        compiler_params=_cp(("parallel",)),
        name="peer_scores",
    )(q, k1, k2)


def _gelu(x):
    return 0.5 * x * (1.0 + lax.erf(x * (2.0 ** -0.5)))


def _peer_dense_kernel(h_ref, u_ref, vt_ref, n1_ref, c1_ref, r2_ref, e2_ref,
                       o_ref, ht_ref, coeff_a_ref, coeff_b_ref, *group_refs, n_i1, blocks_per_tile):
    s = pl.program_id(0)
    NK = PEER_NKEYS
    n_groups = len(group_refs) // 2
    act_refs, acc_refs = group_refs[:n_groups], group_refs[n_groups:]
    rows_a = act_refs[0].shape[0]
    rows_c = acc_refs[0].shape[0]
    per_group = n_i1 // n_groups

    @pl.when(s == 0)
    def _():
        coeff_b_ref[...] = jnp.zeros(coeff_b_ref.shape, BF16)

    @pl.when(s % blocks_per_tile == 0)
    def _():
        ht_ref[...] = h_ref[...].astype(F32).T.astype(BF16)

    @pl.when((s == 0) | (s % blocks_per_tile == 1))
    def _():
        for acc_ref in acc_refs:
            acc_ref[...] = jnp.zeros(acc_ref.shape, F32)

    def activations(g):
        act_refs[g][...] = jnp.dot(u_ref[g * rows_a:(g + 1) * rows_a, :], ht_ref[...], preferred_element_type=F32)

    def accumulate(g, coeff_prev_ref):
        acc_refs[g][...] += jnp.dot(vt_ref[g * rows_c:(g + 1) * rows_c, :], coeff_prev_ref[...],
                                    preferred_element_type=F32)

    def coefficients(il, coeff_cur_ref):
        w = None
        for h in range(PEER_HEADS):
            n1 = n1_ref[h, il:il + 1, :]
            c1 = c1_ref[h, il:il + 1, :]
            term = jnp.where(r2_ref[h] < n1, e2_ref[h] * c1, 0.0)
            w = term if w is None else w + term
        r0 = (il % per_group) * NK
        act = act_refs[il // per_group][r0:r0 + NK, :]
        coeff_cur_ref[il * NK:(il + 1) * NK, :] = (w * _gelu(act)).astype(BF16)

    def body(coeff_cur_ref, coeff_prev_ref):
        activations(0)
        for g in range(1, n_groups + 1):
            if g < n_groups:
                activations(g)
            accumulate(g - 1, coeff_prev_ref)
            for il in range((g - 1) * per_group, g * per_group):
                coefficients(il, coeff_cur_ref)

    @pl.when(s % 2 == 0)
    def _():
        body(coeff_a_ref, coeff_b_ref)

    @pl.when(s % 2 == 1)
    def _():
        body(coeff_b_ref, coeff_a_ref)

    @pl.when((s > 0) & (s % blocks_per_tile == 0))
    def _():
        for g, acc_ref in enumerate(acc_refs):
            o_ref[:, g * rows_c:(g + 1) * rows_c] = acc_ref[...].T


def _peer_dense(h, u, vt, n1, c1, r2, e2, tm, te):
    T, D = h.shape
    H, NK = PEER_HEADS, PEER_NKEYS
    n_i1 = te // NK
    bpt = PEER_N // te
    last = (T // tm) * bpt - 1
    cur = lambda s: jnp.minimum(s, last)
    prev = lambda s: jnp.maximum(s - 1, 0)
    row_spec = pl.BlockSpec((H, n_i1, tm), lambda s: (0, cur(s) % bpt, cur(s) // bpt))
    all_spec = pl.BlockSpec((H, NK, tm), lambda s: (0, 0, cur(s) // bpt))
    return pl.pallas_call(
        functools.partial(_peer_dense_kernel, n_i1=n_i1, blocks_per_tile=bpt),
        grid=(last + 2,),
        in_specs=[pl.BlockSpec((tm, D), lambda s: (cur(s) // bpt, 0)),
                  pl.BlockSpec((te, D), lambda s: (cur(s) % bpt, 0)),
                  pl.BlockSpec((D, te), lambda s: (0, prev(s) % bpt)),
                  row_spec, row_spec, all_spec, all_spec],
        out_specs=pl.BlockSpec((tm, D), lambda s: (prev(s) // bpt, 0)),
        out_shape=jax.ShapeDtypeStruct((T, D), F32),
        scratch_shapes=([pltpu.VMEM((D, tm), BF16), pltpu.VMEM((te, tm), BF16), pltpu.VMEM((te, tm), BF16)]
                        + [pltpu.VMEM((te // PEER_DENSE_GROUPS, tm), F32)] * PEER_DENSE_GROUPS
                        + [pltpu.VMEM((D // PEER_DENSE_GROUPS, tm), F32)] * PEER_DENSE_GROUPS),
        compiler_params=_cp(("arbitrary",)),
        name="peer_dense",
    )(h, u, vt, n1, c1, r2, e2)


def _pad_heads(w, heads, dim, dim_pad):
    lead = w.shape[:-1]
    w = w.reshape(lead + (heads, dim))
    w = jnp.pad(w, [(0, 0)] * len(lead) + [(0, 0), (0, dim_pad - dim)])
    return w.reshape(lead + (heads * dim_pad,))


def _layout_w_in(w_in):
    pool = w_in[..., 0:512]
    q = _pad_heads(w_in[..., 512:896], GLA_HEADS, GLA_DK, GLA_DK_PAD)
    k = _pad_heads(w_in[..., 896:1280], GLA_HEADS, GLA_DK, GLA_DK_PAD)
    v = _pad_heads(w_in[..., 1280:2048], GLA_HEADS, GLA_DV, GLA_DV_PAD)
    g = _pad_heads(w_in[..., 2048:2816], GLA_HEADS, GLA_DV, GLA_DV_PAD)
    ad = jnp.pad(w_in[..., 2816:2832], [(0, 0)] * (w_in.ndim - 1) + [(0, LANES - GLA_LORA)])
    rw = w_in[..., 2832:5392]
    return jnp.concatenate([rw, pool, q, k, v, g, ad], axis=-1).astype(BF16)


def kernel(x, c, ada_w, ada_b, norm1_g, w_in, pool_w, pool_scale, gla_alpha_up, gla_alpha_b, gla_norm_g, rwkv_mu, rwkv_w0, rwkv_w2, rwkv_a0, rwkv_a2, rwkv_g2, rwkv_kk, rwkv_ka, rwkv_rk, rwkv_lnx_g, rwkv_lnx_b, w_out, norm2_g, peer_wq, peer_k1, peer_k2, peer_u, peer_v, final_g):
    B, S, D = x.shape
    L = ada_w.shape[0]
    T = B * S
    tm = min(512, S)
    tpb = S // tm
    tm_mm = min(1024, S)
    tn_mm = 1024
    W = RWKV_WIDTH

    mod = _modulation(c, ada_w, ada_b)
    x2 = x.reshape(T, D)
    ffn = None
    for l in range(L):
        mod3 = mod[l].reshape(B, 1, 6 * D)
        if l == 0:
            h = _normmod(x2, norm1_g[l], mod3, 0, 1, tpb, tm)
        else:
            x2, h = _resid_normmod(x2, ffn, mod[l - 1].reshape(B, 1, 6 * D), 5, norm1_g[l], mod3, 0, 1, tpb, tm)
        proj = _matmul(h, _layout_w_in(w_in[l]), tm_mm, 896)
        y_pool = _pool(proj, pool_w[l], pool_scale[l], B, S, tm)
        up_p = jnp.pad(_pad_heads(gla_alpha_up[l], GLA_HEADS, GLA_DK, GLA_DK_PAD),
                       ((0, LANES - GLA_LORA), (0, 0)))
        ab_p = _pad_heads(gla_alpha_b[l], GLA_HEADS, GLA_DK, GLA_DK_PAD).reshape(1, -1)
        ng_p = _pad_heads(gla_norm_g[l], GLA_HEADS, GLA_DV, GLA_DV_PAD).reshape(1, -1)
        y_gla = _gla(proj, up_p, ab_p, ng_p, B, S)
        zeros = jnp.zeros((RWKV_HEAD_DIM, W), F32)
        w2p = jnp.concatenate([rwkv_w2[l], zeros], axis=0).astype(BF16)
        a2p = jnp.concatenate([zeros, rwkv_a2[l]], axis=0).astype(BF16)
        y_rwkv = _rwkv(proj, rwkv_mu[l].reshape(1, -1), rwkv_w0[l].reshape(1, W), w2p,
                       rwkv_a0[l].reshape(1, W), a2p, rwkv_g2[l].astype(BF16),
                       rwkv_kk[l].reshape(1, W), rwkv_ka[l].reshape(1, W), rwkv_rk[l].reshape(1, W),
                       rwkv_lnx_g[l].reshape(1, W), rwkv_lnx_b[l].reshape(1, W), B, S)
        wo = w_out[l]
        wo_g = jnp.pad(wo[512:1280].reshape(GLA_HEADS, GLA_DV, D),
                       ((0, 0), (0, GLA_DV_PAD - GLA_DV), (0, 0))).reshape(GLA_V_PAD, D)
        x2 = _outproj(y_pool, y_gla, y_rwkv, wo[0:512].astype(BF16), wo_g.astype(BF16),
                      wo[1280:].astype(BF16), x2, mod3, 2 * D // tn_mm, S // tm_mm, tm_mm, tn_mm)
        h = _normmod(x2, norm2_g[l], mod3, 3, 4, tpb, tm)
        q = _matmul(h, peer_wq[l].astype(BF16), tm_mm, tn_mm)
        n1, c1, r2, e2 = _peer_scores(q, peer_k1[l], peer_k2[l], LANES)
        ffn = _peer_dense(h, peer_u[l].astype(BF16), peer_v[l].T.astype(BF16), n1, c1, r2, e2, tm, 1024)
    out = _resid_final(x2, ffn, mod[L - 1].reshape(B, 1, 6 * D), 5, final_g, tpb, tm)
    return out.reshape(B, S, D)
```

```python
import functools

import jax
import jax.numpy as jnp
import numpy as np
from jax import lax
from jax.experimental import pallas as pl
from jax.experimental.pallas import tpu as pltpu

F32 = jnp.float32
BF16 = jnp.bfloat16

LANES = 128
NORM_EPS = 1e-6

POOL_WINDOWS = (2, 4, 8, 16)
POOL_GROUP_DIM = 128
POOL_WIDTH = 512
POOL_HALO = 16

GLA_HEADS = 4
GLA_DK = 96
GLA_DV = 192
GLA_DK_PAD = 128
GLA_DV_PAD = 256
GLA_LORA = 16
GLA_TAU = 16.0
GLA_CHUNK = 64
GLA_SUB = 16
GLA_QK_PAD = GLA_HEADS * GLA_DK_PAD
GLA_V_PAD = GLA_HEADS * GLA_DV_PAD

RWKV_WIDTH = 768
RWKV_HEAD_DIM = 64
RWKV_PAIRS = RWKV_WIDTH // LANES
RWKV_IN = 2560
RWKV_CHUNK = 64
RWKV_LNX_EPS = 64e-5

COL_RWKV = 0
COL_POOL = 2560
COL_Q = 3072
COL_K = 3584
COL_V = 4096
COL_G = 5120
COL_AD = 6144
IN_PAD = 6272

PEER_HEADS = 8
PEER_NKEYS = 128
PEER_HALF = 128
PEER_TOPK = 16
PEER_N = PEER_NKEYS * PEER_NKEYS
PEER_DENSE_GROUPS = 4

NN = (((1,), (0,)), ((), ()))
NT = (((1,), (1,)), ((), ()))

VMEM_LIMIT = 56 * 1024 * 1024


def _cp(sem):
    return pltpu.CompilerParams(dimension_semantics=sem, vmem_limit_bytes=VMEM_LIMIT)


def _dot(a, b, dims=NN):
    return lax.dot_general(a.astype(BF16), b.astype(BF16), dims, preferred_element_type=F32)


def _split(x):
    hi = x.astype(BF16)
    lo = (x - hi.astype(F32)).astype(BF16)
    return hi, lo


def _dot_l2(a, b, dims=NN):
    hi, lo = _split(a)
    bb = b.astype(BF16)
    return (lax.dot_general(hi, bb, dims, preferred_element_type=F32)
            + lax.dot_general(lo, bb, dims, preferred_element_type=F32))


def _dot_r2(a, b):
    hi, lo = _split(b)
    ab = a.astype(BF16)
    return (lax.dot_general(ab, hi, NN, preferred_element_type=F32)
            + lax.dot_general(ab, lo, NN, preferred_element_type=F32))


def _iota(shape, dim):
    return lax.broadcasted_iota(jnp.int32, shape, dim)


def _tri_incl(n):
    return (_iota((n, n), 0) >= _iota((n, n), 1)).astype(BF16)


def _sigmoid(x):
    return 1.0 / (1.0 + jnp.exp(-x))


def _softplus(x):
    return jnp.maximum(x, 0.0) + jnp.log1p(jnp.exp(-jnp.abs(x)))


def _mod_kernel(c_ref, w_ref, b_ref, o_ref):
    c = c_ref[...]
    ca = c * _sigmoid(c)
    o_ref[...] = jnp.dot(ca, w_ref[...], precision=lax.Precision.HIGHEST,
                         preferred_element_type=F32) + b_ref[...]


def _modulation(c, ada_w, ada_b):
    L, D, N6 = ada_w.shape
    B = c.shape[0]
    tn = 1024
    return pl.pallas_call(
        _mod_kernel,
        grid=(L, N6 // tn),
        in_specs=[pl.BlockSpec((B, D), lambda l, j: (0, 0)),
                  pl.BlockSpec((None, D, tn), lambda l, j: (l, 0, j)),
                  pl.BlockSpec((None, 1, tn), lambda l, j: (l, 0, j))],
        out_specs=pl.BlockSpec((None, B, tn), lambda l, j: (l, 0, j)),
        out_shape=jax.ShapeDtypeStruct((L, B, N6), F32),
        compiler_params=_cp(("parallel", "parallel")),
        name="adaln_mod",
    )(c, ada_w, ada_b.reshape(L, 1, N6))


def _normmod_kernel(x_ref, g_ref, sh_ref, sc_ref, o_ref):
    x = x_ref[...]
    y = x * lax.rsqrt(jnp.mean(x * x, axis=-1, keepdims=True) + NORM_EPS) * g_ref[...]
    o_ref[...] = (y * (1.0 + sc_ref[...]) + sh_ref[...]).astype(o_ref.dtype)


def _normmod(x2, g, mod3, sh_blk, sc_blk, tiles_per_batch, tm):
    T, D = x2.shape
    return pl.pallas_call(
        _normmod_kernel,
        grid=(T // tm,),
        in_specs=[pl.BlockSpec((tm, D), lambda i: (i, 0)),
                  pl.BlockSpec((1, D), lambda i: (0, 0)),
                  pl.BlockSpec((None, 1, D), lambda i: (i // tiles_per_batch, 0, sh_blk)),
                  pl.BlockSpec((None, 1, D), lambda i: (i // tiles_per_batch, 0, sc_blk))],
        out_specs=pl.BlockSpec((tm, D), lambda i: (i, 0)),
        out_shape=jax.ShapeDtypeStruct((T, D), BF16),
        compiler_params=_cp(("parallel",)),
        name="rmsnorm_adaln",
    )(x2, g.reshape(1, D), mod3, mod3)


def _resid_normmod_kernel(x_ref, f_ref, gate_ref, g_ref, sh_ref, sc_ref, xo_ref, h_ref):
    x = x_ref[...] + gate_ref[...] * f_ref[...]
    xo_ref[...] = x
    y = x * lax.rsqrt(jnp.mean(x * x, axis=-1, keepdims=True) + NORM_EPS) * g_ref[...]
    h_ref[...] = (y * (1.0 + sc_ref[...]) + sh_ref[...]).astype(h_ref.dtype)


def _resid_normmod(x2, f, mod_prev3, gate_blk, g, mod3, sh_blk, sc_blk, tiles_per_batch, tm):
    T, D = x2.shape
    row = lambda blk: pl.BlockSpec((None, 1, D), lambda i: (i // tiles_per_batch, 0, blk))
    tile = pl.BlockSpec((tm, D), lambda i: (i, 0))
    return pl.pallas_call(
        _resid_normmod_kernel,
        grid=(T // tm,),
        in_specs=[tile, tile, row(gate_blk), pl.BlockSpec((1, D), lambda i: (0, 0)), row(sh_blk), row(sc_blk)],
        out_specs=[tile, tile],
        out_shape=[jax.ShapeDtypeStruct((T, D), F32), jax.ShapeDtypeStruct((T, D), BF16)],
        compiler_params=_cp(("parallel",)),
        name="residual_rmsnorm_adaln",
    )(x2, f, mod_prev3, g.reshape(1, D), mod3, mod3)


def _resid_final_kernel(x_ref, f_ref, gate_ref, g_ref, o_ref):
    x = x_ref[...] + gate_ref[...] * f_ref[...]
    o_ref[...] = x * lax.rsqrt(jnp.mean(x * x, axis=-1, keepdims=True) + NORM_EPS) * g_ref[...]


def _resid_final(x2, f, mod_prev3, gate_blk, g, tiles_per_batch, tm):
    T, D = x2.shape
    tile = pl.BlockSpec((tm, D), lambda i: (i, 0))
    return pl.pallas_call(
        _resid_final_kernel,
        grid=(T // tm,),
        in_specs=[tile, tile,
                  pl.BlockSpec((None, 1, D), lambda i: (i // tiles_per_batch, 0, gate_blk)),
                  pl.BlockSpec((1, D), lambda i: (0, 0))],
        out_specs=tile,
        out_shape=jax.ShapeDtypeStruct((T, D), F32),
        compiler_params=_cp(("parallel",)),
        name="residual_final_rmsnorm",
    )(x2, f, mod_prev3, g.reshape(1, D))


def _matmul_kernel(a_ref, w_ref, o_ref):
    o_ref[...] = jnp.dot(a_ref[...], w_ref[...], preferred_element_type=F32)


def _matmul(a, w, tm, tn):
    M, K = a.shape
    N = w.shape[1]
    return pl.pallas_call(
        _matmul_kernel,
        grid=(M // tm, N // tn),
        in_specs=[pl.BlockSpec((tm, K), lambda i, j: (i, 0)),
                  pl.BlockSpec((K, tn), lambda i, j: (0, j))],
        out_specs=pl.BlockSpec((tm, tn), lambda i, j: (i, j)),
        out_shape=jax.ShapeDtypeStruct((M, N), F32),
        compiler_params=_cp(("parallel", "parallel")),
        name="projection",
    )(a, w)


def _outproj_kernel(yp_ref, yg_ref, yr_ref, wp_ref, wg_ref, wr_ref, x_ref, gate_ref, o_ref):
    acc = jnp.dot(yp_ref[...], wp_ref[...], preferred_element_type=F32)
    acc += jnp.dot(yg_ref[...], wg_ref[...], preferred_element_type=F32)
    acc += jnp.dot(yr_ref[...], wr_ref[...], preferred_element_type=F32)
    o_ref[...] = x_ref[...] + gate_ref[...] * acc


def _outproj(yp, yg, yr, wp, wg, wr, x2, mod3, gate_blk0, tiles_per_batch, tm, tn):
    T, D = x2.shape
    return pl.pallas_call(
        _outproj_kernel,
        grid=(T // tm, D // tn),
        in_specs=[pl.BlockSpec((tm, yp.shape[1]), lambda i, j: (i, 0)),
                  pl.BlockSpec((tm, yg.shape[1]), lambda i, j: (i, 0)),
                  pl.BlockSpec((tm, yr.shape[1]), lambda i, j: (i, 0)),
                  pl.BlockSpec((wp.shape[0], tn), lambda i, j: (0, j)),
                  pl.BlockSpec((wg.shape[0], tn), lambda i, j: (0, j)),
                  pl.BlockSpec((wr.shape[0], tn), lambda i, j: (0, j)),
                  pl.BlockSpec((tm, tn), lambda i, j: (i, j)),
                  pl.BlockSpec((None, 1, tn), lambda i, j: (i // tiles_per_batch, 0, gate_blk0 + j))],
        out_specs=pl.BlockSpec((tm, tn), lambda i, j: (i, j)),
        out_shape=jax.ShapeDtypeStruct((T, D), F32),
        compiler_params=_cp(("parallel", "parallel")),
        name="out_projection",
    )(yp, yg, yr, wp, wg, wr, x2, mod3)


def _pool_kernel(p_ref, w_ref, scale_ref, o_ref, ext_ref, *, ts):
    i = pl.program_id(1)

    @pl.when(i == 0)
    def _():
        ext_ref[0:POOL_HALO, :] = jnp.zeros((POOL_HALO, POOL_WIDTH), F32)

    p = p_ref[...]
    ext_ref[POOL_HALO:POOL_HALO + ts, :] = p
    pos = (i * ts + _iota((ts, 1), 0) + 1).astype(F32)
    for g, win in enumerate(POOL_WINDOWS):
        lo = g * POOL_GROUP_DIM
        acc = p[:, lo:lo + POOL_GROUP_DIM]
        for j in range(1, win):
            acc = acc + ext_ref[POOL_HALO - j:POOL_HALO - j + ts, lo:lo + POOL_GROUP_DIM]
        d = acc / jnp.minimum(pos, float(win)) - p[:, lo:lo + POOL_GROUP_DIM]
        y = _dot(d, w_ref[g]) * scale_ref[:, lo:lo + POOL_GROUP_DIM]
        o_ref[:, lo:lo + POOL_GROUP_DIM] = y.astype(o_ref.dtype)
    ext_ref[0:POOL_HALO, :] = ext_ref[ts:ts + POOL_HALO, :]


def _pool(proj, pool_w, pool_scale, B, S, ts):
    T = B * S
    nt = S // ts
    return pl.pallas_call(
        functools.partial(_pool_kernel, ts=ts),
        grid=(B, nt),
        in_specs=[pl.BlockSpec((ts, POOL_WIDTH), lambda b, i: (b * nt + i, COL_POOL // POOL_WIDTH)),
                  pl.BlockSpec(pool_w.shape, lambda b, i: (0, 0, 0)),
                  pl.BlockSpec((1, POOL_WIDTH), lambda b, i: (0, 0))],
        out_specs=pl.BlockSpec((ts, POOL_WIDTH), lambda b, i: (b * nt + i, 0)),
        out_shape=jax.ShapeDtypeStruct((T, POOL_WIDTH), BF16),
        scratch_shapes=[pltpu.VMEM((ts + POOL_HALO, POOL_WIDTH), F32)],
        compiler_params=_cp(("arbitrary", "arbitrary")),
        name="causal_pool",
    )(proj, pool_w, pool_scale.reshape(1, POOL_WIDTH))


def _gla_kernel(q_ref, k_ref, v_ref, g_ref, ad_ref, up_ref, ab_ref, ng_ref, o_ref, st_ref):
    C, SUB = GLA_CHUNK, GLA_SUB

    @pl.when(pl.program_id(1) == 0)
    def _():
        st_ref[...] = jnp.zeros(st_ref.shape, F32)

    z = _dot(ad_ref[...], up_ref[...]) + ab_ref[...]
    la = -_softplus(-z) * (1.0 / GLA_TAU)
    bcum = _dot_r2(_tri_incl(C), la)
    rowc = _iota((C, 1), 0)
    causal = _iota((C, C), 1) <= _iota((C, C), 0)
    scale = GLA_DK ** -0.5
    H = range(GLA_HEADS)
    subs = range(C // SUB)
    ks = [slice(h * GLA_DK_PAD, (h + 1) * GLA_DK_PAD) for h in H]
    vs = [slice(h * GLA_DV_PAD, (h + 1) * GLA_DV_PAD) for h in H]
    q = [q_ref[:, ks[h]] * scale for h in H]
    k = [k_ref[:, ks[h]] for h in H]
    v = [v_ref[:, vs[h]] for h in H]
    b = [bcum[:, ks[h]] for h in H]
    st = [st_ref[h] for h in H]
    refs = [[b[h][s * SUB - 1:s * SUB, :] if s else jnp.zeros((1, GLA_DK_PAD), F32) for s in subs] for h in H]
    in_sub = [(rowc >= s * SUB) & (rowc < (s + 1) * SUB) for s in subs]
    q_bd = [jnp.concatenate([jnp.where(in_sub[s], q[h] * jnp.exp(b[h] - refs[h][s]), 0.0) for s in subs], axis=1)
            for h in H]
    k_cat = [jnp.concatenate([k[h] * jnp.exp(jnp.where(rowc < (s + 1) * SUB, refs[h][s] - b[h], 0.0)) for s in subs],
                             axis=1) for h in H]
    att = [jnp.where(causal, _dot(q_bd[h], k_cat[h], NT), 0.0) for h in H]
    out = [_dot(q[h] * jnp.exp(b[h]), st[h], NT) + _dot(att[h], v[h]) for h in H]
    b_last = [x[C - 1:C, :] for x in b]
    zpad_v = jnp.zeros((GLA_DV_PAD - C, GLA_DV_PAD), F32)
    zpad_k = jnp.zeros((GLA_DV_PAD - C, GLA_DK_PAD), F32)
    upd = [_dot(jnp.concatenate([v[h], zpad_v], axis=0).T,
                jnp.concatenate([k[h] * jnp.exp(b_last[h] - b[h]), zpad_k], axis=0)) for h in H]
    for h in H:
        st_ref[h] = st[h] * jnp.exp(b_last[h]) + upd[h]
    for h in H:
        ms = jnp.sum(out[h] * out[h], axis=-1, keepdims=True) * (1.0 / GLA_DV)
        o = out[h] * lax.rsqrt(ms + NORM_EPS) * ng_ref[:, vs[h]]
        gg = g_ref[:, vs[h]]
        o_ref[:, vs[h]] = (o * (gg * _sigmoid(gg))).astype(o_ref.dtype)


def _gla(proj, up_p, ab_p, ng_p, B, S):
    T = B * S
    C = GLA_CHUNK
    nc = S // C
    return pl.pallas_call(
        _gla_kernel,
        grid=(B, nc),
        in_specs=[pl.BlockSpec((C, GLA_QK_PAD), lambda b, i: (b * nc + i, COL_Q // GLA_QK_PAD)),
                  pl.BlockSpec((C, GLA_QK_PAD), lambda b, i: (b * nc + i, COL_K // GLA_QK_PAD)),
                  pl.BlockSpec((C, GLA_V_PAD), lambda b, i: (b * nc + i, COL_V // GLA_V_PAD)),
                  pl.BlockSpec((C, GLA_V_PAD), lambda b, i: (b * nc + i, COL_G // GLA_V_PAD)),
                  pl.BlockSpec((C, LANES), lambda b, i: (b * nc + i, COL_AD // LANES)),
                  pl.BlockSpec((LANES, GLA_QK_PAD), lambda b, i: (0, 0)),
                  pl.BlockSpec((1, GLA_QK_PAD), lambda b, i: (0, 0)),
                  pl.BlockSpec((1, GLA_V_PAD), lambda b, i: (0, 0))],
        out_specs=pl.BlockSpec((C, GLA_V_PAD), lambda b, i: (b * nc + i, 0)),
        out_shape=jax.ShapeDtypeStruct((T, GLA_V_PAD), BF16),
        scratch_shapes=[pltpu.VMEM((GLA_HEADS, GLA_DV_PAD, GLA_DK_PAD), F32)],
        compiler_params=_cp(("arbitrary", "arbitrary")),
        name="gla_chunked",
    )(proj, proj, proj, proj, proj, up_p, ab_p, ng_p)


def _rwkv_kernel(y_ref, mu_ref, w0_ref, w2_ref, a0_ref, a2_ref, g2_ref, kk_ref, ka_ref, rk_ref,
                 lg_ref, lb_ref, o_ref, st_ref, prev_ref):
    C = RWKV_CHUNK
    W = RWKV_WIDTH

    @pl.when(pl.program_id(1) == 0)
    def _():
        st_ref[...] = jnp.zeros(st_ref.shape, F32)
        prev_ref[...] = jnp.zeros(prev_ref.shape, F32)

    y = y_ref[...]
    shifted = jnp.where(_iota((C, 1), 0) == 0, prev_ref[...], pltpu.roll(y, 1, 0))
    prev_ref[...] = y[C - 1:C, :]
    y = y + (shifted - y) * mu_ref[...]
    r_all = y[:, 0:W]
    k_all = y[:, W:2 * W]
    v_all = y[:, 2 * W:3 * W]
    wdad = y[:, 3 * W:3 * W + LANES]
    gd = y[:, 3 * W + LANES:3 * W + 2 * LANES]
    w_all = -_softplus(-(w0_ref[...] + _dot(jnp.tanh(wdad), w2_ref[...]))) - 0.5
    ld_all = -jnp.exp(w_all)
    a_all = _sigmoid(a0_ref[...] + _dot(wdad, a2_ref[...]))
    gate_all = _dot(_sigmoid(gd), g2_ref[...])

    lane = _iota((1, LANES), 1)
    m0 = (lane < RWKV_HEAD_DIM).astype(F32)
    m1 = 1.0 - m0
    seg = (_iota((LANES, LANES), 0) < RWKV_HEAD_DIM) == (_iota((LANES, LANES), 1) < RWKV_HEAD_DIM)
    seg_ones = seg.astype(BF16)
    ri = _iota((2 * C, 2 * C), 0)
    ci = _iota((2 * C, 2 * C), 1)
    same = (ri < C) == (ci < C)
    strict = same & (ci < ri)
    incl = same & (ci <= ri)
    eye = (ri == ci).astype(F32)
    tri = _tri_incl(C)

    def stack(x):
        return jnp.concatenate([x * m0, x * m1], axis=0)

    P = range(RWKV_PAIRS)
    lanes = [slice(p * LANES, (p + 1) * LANES) for p in P]
    r = [r_all[:, s] for s in lanes]
    v = [v_all[:, s] for s in lanes]
    a = [a_all[:, s] for s in lanes]
    kk = [k_all[:, s] * kk_ref[:, s] for s in lanes]
    nrm = [jnp.sqrt(_dot_l2(x * x, seg_ones)) for x in kk]
    kk = [kk[p] / jnp.maximum(nrm[p], 1e-12) for p in P]
    k = [k_all[:, lanes[p]] * (1.0 + (a[p] - 1.0) * ka_ref[:, lanes[p]]) for p in P]
    beta = [kk[p] * a[p] for p in P]

    cum_all = _dot_r2(tri, ld_all)
    cum = [cum_all[:, s] for s in lanes]
    c_last = [x[C - 1:C, :] for x in cum]
    at = [-kk[p] * jnp.exp(cum[p] - ld_all[:, lanes[p]]) for p in P]
    rt = [r[p] * jnp.exp(cum[p]) for p in P]
    pinv = [jnp.exp(-x) for x in cum]
    bt = [beta[p] * pinv[p] for p in P]
    kt = [k[p] * pinv[p] for p in P]
    dlast = [jnp.exp(c_last[p] - cum[p]) for p in P]
    ar2 = [jnp.concatenate([stack(at[p]), stack(rt[p])], axis=0) for p in P]
    bk2 = [jnp.concatenate([bt[p], bt[p], kt[p], kt[p]], axis=0) for p in P]
    v2 = [stack(x) for x in v]
    g = [st_ref[p] for p in P]

    sc = [_dot(ar2[p], bk2[p], NT) for p in P]
    a_ab = [jnp.where(strict, x[0:2 * C, 0:2 * C], 0.0) for x in sc]
    a_ak = [jnp.where(strict, x[0:2 * C, 2 * C:4 * C], 0.0) for x in sc]
    l_rbk = [jnp.concatenate([jnp.where(incl, x[2 * C:4 * C, 0:2 * C], 0.0),
                              jnp.where(incl, x[2 * C:4 * C, 2 * C:4 * C], 0.0)], axis=1) for x in sc]
    x0 = [_dot(ar2[p], g[p], NT) for p in P]
    rhs = [x0[p][0:2 * C, :] + _dot(a_ak[p], v2[p]) for p in P]
    t = [eye + x for x in a_ab]
    m = [_dot(x, x) for x in a_ab]
    for _ in range(int(np.log2(C)) - 2):
        tm = [_dot(jnp.concatenate([t[p], m[p]], axis=0), m[p]) for p in P]
        t = [t[p] + tm[p][0:2 * C, :] for p in P]
        m = [x[2 * C:4 * C, :] for x in tm]
    t = [t[p] + _dot(t[p], m[p]) for p in P]
    u2 = [_dot(t[p], rhs[p]) for p in P]
    uv = [jnp.concatenate([u2[p], v2[p]], axis=0) for p in P]
    o2 = [x0[p][2 * C:4 * C, :] + _dot(l_rbk[p], uv[p]) for p in P]
    o = [x[0:C, :] + x[C:2 * C, :] for x in o2]
    bkd = [jnp.concatenate([stack(beta[p] * dlast[p]), stack(k[p] * dlast[p])], axis=0) for p in P]
    upd = [_dot(jnp.concatenate([u2[p].T, v2[p].T], axis=1), bkd[p]) for p in P]
    for p in P:
        st_ref[p] = g[p] * jnp.exp(c_last[p]) + jnp.where(seg, upd[p], 0.0)

    mean = [_dot_l2(x, seg_ones) * (1.0 / RWKV_HEAD_DIM) for x in o]
    oc = [o[p] - mean[p] for p in P]
    var = [_dot_l2(x * x, seg_ones) * (1.0 / RWKV_HEAD_DIM) for x in oc]
    bonus = [_dot_l2(r[p] * k[p] * rk_ref[:, lanes[p]], seg_ones) * v[p] for p in P]
    for p in P:
        s = lanes[p]
        on = oc[p] * lax.rsqrt(var[p] + RWKV_LNX_EPS) * lg_ref[:, s] + lb_ref[:, s]
        o_ref[:, s] = ((on + bonus[p]) * gate_all[:, s]).astype(o_ref.dtype)


def _rwkv(proj, mu, w0, w2p, a0, a2p, g2, kk, ka, rk, lg, lb, B, S):
    T = B * S
    C = RWKV_CHUNK
    nc = S // C
    W = RWKV_WIDTH
    row = lambda n: pl.BlockSpec((1, n), lambda b, i: (0, 0))
    mat = lambda: pl.BlockSpec((LANES, W), lambda b, i: (0, 0))
    return pl.pallas_call(
        _rwkv_kernel,
        grid=(B, nc),
        in_specs=[pl.BlockSpec((C, RWKV_IN), lambda b, i: (b * nc + i, COL_RWKV // RWKV_IN)),
                  row(RWKV_IN), row(W), mat(), row(W), mat(), mat(), row(W), row(W), row(W), row(W), row(W)],
        out_specs=pl.BlockSpec((C, W), lambda b, i: (b * nc + i, 0)),
        out_shape=jax.ShapeDtypeStruct((T, W), BF16),
        scratch_shapes=[pltpu.VMEM((RWKV_PAIRS, LANES, LANES), F32),
                        pltpu.VMEM((1, RWKV_IN), F32)],
        compiler_params=_cp(("arbitrary", "arbitrary")),
        name="rwkv7_chunked",
    )(proj, mu, w0, w2p, a0, a2p, g2, kk, ka, rk, lg, lb)


def _take_top(works, n):
    works = list(works)
    rows = _iota(works[0].shape, 0)
    n_rows = works[0].shape[0]
    ranks = [jnp.full(w.shape, float(n), F32) for w in works]
    vals = [[] for _ in works]
    for i in range(n):
        m = [jnp.max(w, axis=0, keepdims=True) for w in works]
        first = [jnp.min(jnp.where(w == mm, rows, n_rows), axis=0, keepdims=True) for w, mm in zip(works, m)]
        hit = [rows == f for f in first]
        works = [jnp.where(h, -jnp.inf, w) for h, w in zip(hit, works)]
        ranks = [jnp.where(h, float(i), r) for h, r in zip(hit, ranks)]
        for v, mm in zip(vals, m):
            v.append(mm)
    return vals, ranks


def _batcher_network(n):
    pairs = []
    p = 1
    while p < n:
        k = p
        while k >= 1:
            for j in range(k % p, n - k, 2 * k):
                for i in range(min(k, n - j - k)):
                    if (i + j) // (2 * p) == (i + j + k) // (2 * p):
                        pairs.append((i + j, i + j + k))
            k //= 2
        p *= 2
    return pairs


def _sublane_all(x, op):
    for shift in (4, 2, 1):
        x = op(x, pltpu.roll(x, shift, 0))
    return x


def _sorted_top(pieces):
    p = list(pieces)
    n = len(p)
    for i, j in _batcher_network(n):
        p[i], p[j] = jnp.maximum(p[i], p[j]), jnp.minimum(p[i], p[j])
    for shift in (4, 2, 1):
        p = [jnp.maximum(p[i], pltpu.roll(p[n - 1 - i], shift, 0)) for i in range(n)]
        k = n // 2
        while k >= 1:
            for i in range(n):
                if not i & k:
                    p[i], p[i + k] = jnp.maximum(p[i], p[i + k]), jnp.minimum(p[i], p[i + k])
            k //= 2
    return p


def _peer_pairs():
    K = PEER_TOPK
    pairs = [(a, b) for a in range(K) for b in range(K) if (a + 1) * (b + 1) <= K]
    n_cand = len(pairs) + (-len(pairs) % 8)
    a_idx = _iota((K, n_cand), 0)
    c_idx = _iota((K, n_cand), 1)
    owner = jnp.zeros((K, n_cand), F32)
    for a in range(K):
        first = pairs.index((a, 0))
        owner = jnp.where((a_idx == a) & (c_idx >= first) & (c_idx < first + K // (a + 1)), 1.0, owner)
    return pairs, n_cand, owner


def _peer_score_kernel(q_ref, k1_ref, k2_ref, n1_ref, c1_ref, r2_ref, e2_ref):
    K = PEER_TOPK
    SUB = 8
    H = range(PEER_HEADS)
    lanes = q_ref.shape[0]
    n_pieces = PEER_NKEYS // SUB
    pairs, n_cand, owner = _peer_pairs()
    sub = _iota((SUB, lanes), 0)
    neg = jnp.full((SUB, lanes), -jnp.inf, F32)
    bad = jnp.zeros((SUB, lanes), F32)

    def scores(h):
        base = h * 2 * PEER_HALF
        return (_dot(k1_ref[h], q_ref[:, base:base + PEER_HALF], NT),
                _dot(k2_ref[h], q_ref[:, base + PEER_HALF:base + 2 * PEER_HALF], NT))

    def ambiguous(top, pieces):
        amb = jnp.zeros((SUB, lanes), F32)
        for a in range(len(top) - 1):
            amb = jnp.where(top[a] > top[a + 1], amb, 1.0)
        cnt = jnp.zeros((SUB, lanes), F32)
        for p in pieces:
            cnt = cnt + (p >= top[-1]).astype(F32)
        return jnp.where(_sublane_all(cnt, jnp.add) == len(top), amb, 1.0)

    for h in H:
        s1, s2 = scores(h)
        p1 = [s1[r * SUB:(r + 1) * SUB, :] for r in range(n_pieces)]
        p2 = [s2[r * SUB:(r + 1) * SUB, :] for r in range(n_pieces)]
        t1, t2 = _sorted_top(p1), _sorted_top(p2)
        bad = jnp.maximum(bad, jnp.maximum(ambiguous(t1, p1), ambiguous(t2, p2)))
        cand = []
        for k in range(n_cand // SUB):
            x = neg
            for j, (a, b) in enumerate(pairs[k * SUB:(k + 1) * SUB]):
                x = jnp.where(sub == j, t1[a] + t2[b], x)
            cand.append(x)
        best, work = [], list(cand)
        for _ in range(K):
            m = work[0]
            for w in work[1:]:
                m = jnp.maximum(m, w)
            m = _sublane_all(m, jnp.maximum)
            best.append(m)
            work = [jnp.where(w == m, neg, w) for w in work]
        bad = jnp.maximum(bad, ambiguous(best, cand))
        z = jnp.ones((SUB, lanes), F32)
        for c in best[1:]:
            z = z + jnp.exp(c - best[0])
        sel = jnp.concatenate([(c >= best[-1]).astype(F32) for c in cand], axis=0)
        n_sel = _dot(owner, sel)
        theta = jnp.full((K, lanes), -jnp.inf, F32)
        for b in range(K):
            theta = jnp.where(n_sel == b + 1, -jnp.concatenate([t2[b]] * (K // SUB), axis=0), theta)
        th = [jnp.broadcast_to(theta[a:a + 1, :], (SUB, lanes)) for a in range(K)]
        for r in range(n_pieces):
            rs = slice(r * SUB, (r + 1) * SUB)
            m1 = neg
            for a in range(K):
                m1 = jnp.where(p1[r] == t1[a], th[a], m1)
            n1_ref[h, rs, :] = m1
            c1_ref[h, rs, :] = jnp.exp(p1[r] - t1[0]) / z
            r2_ref[h, rs, :] = -p2[r]
            e2_ref[h, rs, :] = jnp.exp(p2[r] - t2[0])

    @pl.when(jnp.max(bad) > 0.0)
    def _():
        v1, rank1, cand = [], [], []
        for h in H:
            s1, s2 = scores(h)
            (va, vb), (ra, rb) = _take_top([s1, s2], K)
            pad = [jnp.full_like(va[0], -jnp.inf)] * (n_cand - len(pairs))
            cand.append(jnp.concatenate([va[a] + vb[b] for a, b in pairs] + pad, axis=0))
            v1.append(va[0])
            rank1.append(ra)
            c1_ref[h] = s1
            r2_ref[h] = rb
            e2_ref[h] = jnp.exp(s2 - vb[0])
        best, crank = _take_top(cand, K)
        for h in H:
            z = jnp.ones_like(best[h][0])
            for c in best[h][1:]:
                z = z + jnp.exp(c - best[h][0])
            n_sel = _dot(owner, (crank[h] < K).astype(F32))
            n1 = jnp.full_like(rank1[h], -1.0)
            for a in range(K):
                n1 = jnp.where(rank1[h] == a, n_sel[a:a + 1, :] - 1.0, n1)
            n1_ref[h] = n1
            c1_ref[h] = jnp.exp(c1_ref[h] - v1[h]) / z


def _peer_scores(q, k1, k2, tt):
    T = q.shape[0]
    H, NK = PEER_HEADS, PEER_NKEYS
    out = jax.ShapeDtypeStruct((H, NK, T), F32)
    ospec = pl.BlockSpec((H, NK, tt), lambda i: (0, 0, i))
    return pl.pallas_call(
        _peer_score_kernel,
        grid=(T // tt,),
        in_specs=[pl.BlockSpec((tt, q.shape[1]), lambda i: (i, 0)),
                  pl.BlockSpec(k1.shape, lambda i: (0, 0, 0)),
                  pl.BlockSpec(k2.shape, lambda i: (0, 0, 0))],
        out_specs=[ospec, ospec, ospec, ospec],
        out_shape=[out, out, out, out],
        compiler_params=_cp(("parallel",)),
        name="peer_scores",
    )(q, k1, k2)


def _gelu(x):
    return 0.5 * x * (1.0 + lax.erf(x * (2.0 ** -0.5)))


def _peer_dense_kernel(h_ref, u_ref, vt_ref, n1_ref, c1_ref, r2_ref, e2_ref,
                       o_ref, ht_ref, coeff_a_ref, coeff_b_ref, *group_refs, n_i1, blocks_per_tile):
    s = pl.program_id(0)
    NK = PEER_NKEYS
    n_groups = len(group_refs) // 2
    act_refs, acc_refs = group_refs[:n_groups], group_refs[n_groups:]
    rows_a = act_refs[0].shape[0]
    rows_c = acc_refs[0].shape[0]
    per_group = n_i1 // n_groups

    @pl.when(s == 0)
    def _():
        coeff_b_ref[...] = jnp.zeros(coeff_b_ref.shape, BF16)

    @pl.when(s % blocks_per_tile == 0)
    def _():
        ht_ref[...] = h_ref[...].astype(F32).T.astype(BF16)

    @pl.when((s == 0) | (s % blocks_per_tile == 1))
    def _():
        for acc_ref in acc_refs:
            acc_ref[...] = jnp.zeros(acc_ref.shape, F32)

    def activations(g):
        act_refs[g][...] = jnp.dot(u_ref[g * rows_a:(g + 1) * rows_a, :], ht_ref[...], preferred_element_type=F32)

    def accumulate(g, coeff_prev_ref):
        acc_refs[g][...] += jnp.dot(vt_ref[g * rows_c:(g + 1) * rows_c, :], coeff_prev_ref[...],
                                    preferred_element_type=F32)

    def coefficients(il, coeff_cur_ref):
        w = None
        for h in range(PEER_HEADS):
            n1 = n1_ref[h, il:il + 1, :]
            c1 = c1_ref[h, il:il + 1, :]
            term = jnp.where(r2_ref[h] <= n1, e2_ref[h] * c1, 0.0)
            w = term if w is None else w + term
        r0 = (il % per_group) * NK
        act = act_refs[il // per_group][r0:r0 + NK, :]
        coeff_cur_ref[il * NK:(il + 1) * NK, :] = (w * _gelu(act)).astype(BF16)

    def body(coeff_cur_ref, coeff_prev_ref):
        activations(0)
        for g in range(1, n_groups + 1):
            if g < n_groups:
                activations(g)
            accumulate(g - 1, coeff_prev_ref)
            for il in range((g - 1) * per_group, g * per_group):
                coefficients(il, coeff_cur_ref)

    @pl.when(s % 2 == 0)
    def _():
        body(coeff_a_ref, coeff_b_ref)

    @pl.when(s % 2 == 1)
    def _():
        body(coeff_b_ref, coeff_a_ref)

    @pl.when((s > 0) & (s % blocks_per_tile == 0))
    def _():
        for g, acc_ref in enumerate(acc_refs):
            o_ref[:, g * rows_c:(g + 1) * rows_c] = acc_ref[...].T


def _peer_dense(h, u, vt, n1, c1, r2, e2, tm, te):
    T, D = h.shape
    H, NK = PEER_HEADS, PEER_NKEYS
    n_i1 = te // NK
    bpt = PEER_N // te
    last = (T // tm) * bpt - 1
    cur = lambda s: jnp.minimum(s, last)
    prev = lambda s: jnp.maximum(s - 1, 0)
    row_spec = pl.BlockSpec((H, n_i1, tm), lambda s: (0, cur(s) % bpt, cur(s) // bpt))
    all_spec = pl.BlockSpec((H, NK, tm), lambda s: (0, 0, cur(s) // bpt))
    return pl.pallas_call(
        functools.partial(_peer_dense_kernel, n_i1=n_i1, blocks_per_tile=bpt),
        grid=(last + 2,),
        in_specs=[pl.BlockSpec((tm, D), lambda s: (cur(s) // bpt, 0)),
                  pl.BlockSpec((te, D), lambda s: (cur(s) % bpt, 0)),
                  pl.BlockSpec((D, te), lambda s: (0, prev(s) % bpt)),
                  row_spec, row_spec, all_spec, all_spec],
        out_specs=pl.BlockSpec((tm, D), lambda s: (prev(s) // bpt, 0)),
        out_shape=jax.ShapeDtypeStruct((T, D), F32),
        scratch_shapes=([pltpu.VMEM((D, tm), BF16), pltpu.VMEM((te, tm), BF16), pltpu.VMEM((te, tm), BF16)]
                        + [pltpu.VMEM((te // PEER_DENSE_GROUPS, tm), F32)] * PEER_DENSE_GROUPS
                        + [pltpu.VMEM((D // PEER_DENSE_GROUPS, tm), F32)] * PEER_DENSE_GROUPS),
        compiler_params=_cp(("arbitrary",)),
        name="peer_dense",
    )(h, u, vt, n1, c1, r2, e2)


def _pad_heads(w, heads, dim, dim_pad):
    lead = w.shape[:-1]
    w = w.reshape(lead + (heads, dim))
    w = jnp.pad(w, [(0, 0)] * len(lead) + [(0, 0), (0, dim_pad - dim)])
    return w.reshape(lead + (heads * dim_pad,))


def _layout_w_in(w_in):
    pool = w_in[..., 0:512]
    q = _pad_heads(w_in[..., 512:896], GLA_HEADS, GLA_DK, GLA_DK_PAD)
    k = _pad_heads(w_in[..., 896:1280], GLA_HEADS, GLA_DK, GLA_DK_PAD)
    v = _pad_heads(w_in[..., 1280:2048], GLA_HEADS, GLA_DV, GLA_DV_PAD)
    g = _pad_heads(w_in[..., 2048:2816], GLA_HEADS, GLA_DV, GLA_DV_PAD)
    ad = jnp.pad(w_in[..., 2816:2832], [(0, 0)] * (w_in.ndim - 1) + [(0, LANES - GLA_LORA)])
    rw = w_in[..., 2832:5392]
    return jnp.concatenate([rw, pool, q, k, v, g, ad], axis=-1).astype(BF16)


def kernel(x, c, ada_w, ada_b, norm1_g, w_in, pool_w, pool_scale, gla_alpha_up, gla_alpha_b, gla_norm_g, rwkv_mu, rwkv_w0, rwkv_w2, rwkv_a0, rwkv_a2, rwkv_g2, rwkv_kk, rwkv_ka, rwkv_rk, rwkv_lnx_g, rwkv_lnx_b, w_out, norm2_g, peer_wq, peer_k1, peer_k2, peer_u, peer_v, final_g):
    B, S, D = x.shape
    L = ada_w.shape[0]
    T = B * S
    tm = min(512, S)
    tpb = S // tm
    tm_mm = min(1024, S)
    tn_mm = 1024
    W = RWKV_WIDTH

    mod = _modulation(c, ada_w, ada_b)
    x2 = x.reshape(T, D)
    ffn = None
    for l in range(L):
        mod3 = mod[l].reshape(B, 1, 6 * D)
        if l == 0:
            h = _normmod(x2, norm1_g[l], mod3, 0, 1, tpb, tm)
        else:
            x2, h = _resid_normmod(x2, ffn, mod[l - 1].reshape(B, 1, 6 * D), 5, norm1_g[l], mod3, 0, 1, tpb, tm)
        proj = _matmul(h, _layout_w_in(w_in[l]), tm_mm, 896)
        y_pool = _pool(proj, pool_w[l], pool_scale[l], B, S, tm)
        up_p = jnp.pad(_pad_heads(gla_alpha_up[l], GLA_HEADS, GLA_DK, GLA_DK_PAD),
                       ((0, LANES - GLA_LORA), (0, 0)))
        ab_p = _pad_heads(gla_alpha_b[l], GLA_HEADS, GLA_DK, GLA_DK_PAD).reshape(1, -1)
        ng_p = _pad_heads(gla_norm_g[l], GLA_HEADS, GLA_DV, GLA_DV_PAD).reshape(1, -1)
        y_gla = _gla(proj, up_p, ab_p, ng_p, B, S)
        zeros = jnp.zeros((RWKV_HEAD_DIM, W), F32)
        w2p = jnp.concatenate([rwkv_w2[l], zeros], axis=0).astype(BF16)
        a2p = jnp.concatenate([zeros, rwkv_a2[l]], axis=0).astype(BF16)
        y_rwkv = _rwkv(proj, rwkv_mu[l].reshape(1, -1), rwkv_w0[l].reshape(1, W), w2p,
                       rwkv_a0[l].reshape(1, W), a2p, rwkv_g2[l].astype(BF16),
                       rwkv_kk[l].reshape(1, W), rwkv_ka[l].reshape(1, W), rwkv_rk[l].reshape(1, W),
                       rwkv_lnx_g[l].reshape(1, W), rwkv_lnx_b[l].reshape(1, W), B, S)
        wo = w_out[l]
        wo_g = jnp.pad(wo[512:1280].reshape(GLA_HEADS, GLA_DV, D),
                       ((0, 0), (0, GLA_DV_PAD - GLA_DV), (0, 0))).reshape(GLA_V_PAD, D)
        x2 = _outproj(y_pool, y_gla, y_rwkv, wo[0:512].astype(BF16), wo_g.astype(BF16),
                      wo[1280:].astype(BF16), x2, mod3, 2 * D // tn_mm, S // tm_mm, tm_mm, tn_mm)
        h = _normmod(x2, norm2_g[l], mod3, 3, 4, tpb, tm)
        q = _matmul(h, peer_wq[l].astype(BF16), tm_mm, tn_mm)
        n1, c1, r2, e2 = _peer_scores(q, peer_k1[l], peer_k2[l], LANES)
        ffn = _peer_dense(h, peer_u[l].astype(BF16), peer_v[l].T.astype(BF16), n1, c1, r2, e2, tm, 1024)
    out = _resid_final(x2, ffn, mod[L - 1].reshape(B, 1, 6 * D), 5, final_g, tpb, tm)
    return out.reshape(B, S, D)
```

```python
import functools

import jax
import jax.numpy as jnp
import numpy as np
from jax import lax
from jax.experimental import pallas as pl
from jax.experimental.pallas import tpu as pltpu

F32 = jnp.float32
BF16 = jnp.bfloat16

LANES = 128
NORM_EPS = 1e-6

POOL_WINDOWS = (2, 4, 8, 16)
POOL_GROUP_DIM = 128
POOL_WIDTH = 512
POOL_HALO = 16

GLA_HEADS = 4
GLA_DK = 96
GLA_DV = 192
GLA_DK_PAD = 128
GLA_DV_PAD = 256
GLA_LORA = 16
GLA_TAU = 16.0
GLA_CHUNK = 64
GLA_SUB = 16
GLA_QK_PAD = GLA_HEADS * GLA_DK_PAD
GLA_V_PAD = GLA_HEADS * GLA_DV_PAD

RWKV_WIDTH = 768
RWKV_HEAD_DIM = 64
RWKV_PAIRS = RWKV_WIDTH // LANES
RWKV_IN = 2560
RWKV_CHUNK = 64
RWKV_LNX_EPS = 64e-5

COL_RWKV = 0
COL_POOL = 2560
COL_Q = 3072
COL_K = 3584
COL_V = 4096
COL_G = 5120
COL_AD = 6144
IN_PAD = 6272

PEER_HEADS = 8
PEER_NKEYS = 128
PEER_HALF = 128
PEER_TOPK = 16
PEER_N = PEER_NKEYS * PEER_NKEYS
PEER_DENSE_GROUPS = 4

NN = (((1,), (0,)), ((), ()))
NT = (((1,), (1,)), ((), ()))

VMEM_LIMIT = 56 * 1024 * 1024


def _cp(sem):
    return pltpu.CompilerParams(dimension_semantics=sem, vmem_limit_bytes=VMEM_LIMIT)


def _dot(a, b, dims=NN):
    return lax.dot_general(a.astype(BF16), b.astype(BF16), dims, preferred_element_type=F32)


def _split(x):
    hi = x.astype(BF16)
    lo = (x - hi.astype(F32)).astype(BF16)
    return hi, lo


def _dot_l2(a, b, dims=NN):
    hi, lo = _split(a)
    bb = b.astype(BF16)
    return (lax.dot_general(hi, bb, dims, preferred_element_type=F32)
            + lax.dot_general(lo, bb, dims, preferred_element_type=F32))


def _dot_r2(a, b):
    hi, lo = _split(b)
    ab = a.astype(BF16)
    return (lax.dot_general(ab, hi, NN, preferred_element_type=F32)
            + lax.dot_general(ab, lo, NN, preferred_element_type=F32))


def _iota(shape, dim):
    return lax.broadcasted_iota(jnp.int32, shape, dim)


def _tri_incl(n):
    return (_iota((n, n), 0) >= _iota((n, n), 1)).astype(BF16)


def _sigmoid(x):
    return 1.0 / (1.0 + jnp.exp(-x))


def _softplus(x):
    return jnp.maximum(x, 0.0) + jnp.log1p(jnp.exp(-jnp.abs(x)))


def _mod_kernel(c_ref, w_ref, b_ref, o_ref):
    c = c_ref[...]
    ca = c * _sigmoid(c)
    o_ref[...] = jnp.dot(ca, w_ref[...], precision=lax.Precision.HIGHEST,
                         preferred_element_type=F32) + b_ref[...]


def _modulation(c, ada_w, ada_b):
    L, D, N6 = ada_w.shape
    B = c.shape[0]
    tn = 1024
    return pl.pallas_call(
        _mod_kernel,
        grid=(L, N6 // tn),
        in_specs=[pl.BlockSpec((B, D), lambda l, j: (0, 0)),
                  pl.BlockSpec((None, D, tn), lambda l, j: (l, 0, j)),
                  pl.BlockSpec((None, 1, tn), lambda l, j: (l, 0, j))],
        out_specs=pl.BlockSpec((None, B, tn), lambda l, j: (l, 0, j)),
        out_shape=jax.ShapeDtypeStruct((L, B, N6), F32),
        compiler_params=_cp(("parallel", "parallel")),
        name="adaln_mod",
    )(c, ada_w, ada_b.reshape(L, 1, N6))


def _normmod_kernel(x_ref, g_ref, sh_ref, sc_ref, o_ref):
    x = x_ref[...]
    y = x * lax.rsqrt(jnp.mean(x * x, axis=-1, keepdims=True) + NORM_EPS) * g_ref[...]
    o_ref[...] = (y * (1.0 + sc_ref[...]) + sh_ref[...]).astype(o_ref.dtype)


def _normmod(x2, g, mod3, sh_blk, sc_blk, tiles_per_batch, tm):
    T, D = x2.shape
    return pl.pallas_call(
        _normmod_kernel,
        grid=(T // tm,),
        in_specs=[pl.BlockSpec((tm, D), lambda i: (i, 0)),
                  pl.BlockSpec((1, D), lambda i: (0, 0)),
                  pl.BlockSpec((None, 1, D), lambda i: (i // tiles_per_batch, 0, sh_blk)),
                  pl.BlockSpec((None, 1, D), lambda i: (i // tiles_per_batch, 0, sc_blk))],
        out_specs=pl.BlockSpec((tm, D), lambda i: (i, 0)),
        out_shape=jax.ShapeDtypeStruct((T, D), BF16),
        compiler_params=_cp(("parallel",)),
        name="rmsnorm_adaln",
    )(x2, g.reshape(1, D), mod3, mod3)


def _resid_normmod_kernel(x_ref, f_ref, gate_ref, g_ref, sh_ref, sc_ref, xo_ref, h_ref):
    x = x_ref[...] + gate_ref[...] * f_ref[...]
    xo_ref[...] = x
    y = x * lax.rsqrt(jnp.mean(x * x, axis=-1, keepdims=True) + NORM_EPS) * g_ref[...]
    h_ref[...] = (y * (1.0 + sc_ref[...]) + sh_ref[...]).astype(h_ref.dtype)


def _resid_normmod(x2, f, mod_prev3, gate_blk, g, mod3, sh_blk, sc_blk, tiles_per_batch, tm):
    T, D = x2.shape
    row = lambda blk: pl.BlockSpec((None, 1, D), lambda i: (i // tiles_per_batch, 0, blk))
    tile = pl.BlockSpec((tm, D), lambda i: (i, 0))
    return pl.pallas_call(
        _resid_normmod_kernel,
        grid=(T // tm,),
        in_specs=[tile, tile, row(gate_blk), pl.BlockSpec((1, D), lambda i: (0, 0)), row(sh_blk), row(sc_blk)],
        out_specs=[tile, tile],
        out_shape=[jax.ShapeDtypeStruct((T, D), F32), jax.ShapeDtypeStruct((T, D), BF16)],
        compiler_params=_cp(("parallel",)),
        name="residual_rmsnorm_adaln",
    )(x2, f, mod_prev3, g.reshape(1, D), mod3, mod3)


def _resid_final_kernel(x_ref, f_ref, gate_ref, g_ref, o_ref):
    x = x_ref[...] + gate_ref[...] * f_ref[...]
    o_ref[...] = x * lax.rsqrt(jnp.mean(x * x, axis=-1, keepdims=True) + NORM_EPS) * g_ref[...]


def _resid_final(x2, f, mod_prev3, gate_blk, g, tiles_per_batch, tm):
    T, D = x2.shape
    tile = pl.BlockSpec((tm, D), lambda i: (i, 0))
    return pl.pallas_call(
        _resid_final_kernel,
        grid=(T // tm,),
        in_specs=[tile, tile,
                  pl.BlockSpec((None, 1, D), lambda i: (i // tiles_per_batch, 0, gate_blk)),
                  pl.BlockSpec((1, D), lambda i: (0, 0))],
        out_specs=tile,
        out_shape=jax.ShapeDtypeStruct((T, D), F32),
        compiler_params=_cp(("parallel",)),
        name="residual_final_rmsnorm",
    )(x2, f, mod_prev3, g.reshape(1, D))


def _matmul_kernel(a_ref, w_ref, o_ref):
    o_ref[...] = jnp.dot(a_ref[...], w_ref[...], preferred_element_type=F32)


def _matmul(a, w, tm, tn):
    M, K = a.shape
    N = w.shape[1]
    return pl.pallas_call(
        _matmul_kernel,
        grid=(M // tm, N // tn),
        in_specs=[pl.BlockSpec((tm, K), lambda i, j: (i, 0)),
                  pl.BlockSpec((K, tn), lambda i, j: (0, j))],
        out_specs=pl.BlockSpec((tm, tn), lambda i, j: (i, j)),
        out_shape=jax.ShapeDtypeStruct((M, N), F32),
        compiler_params=_cp(("parallel", "parallel")),
        name="projection",
    )(a, w)


def _outproj_kernel(yp_ref, yg_ref, yr_ref, wp_ref, wg_ref, wr_ref, x_ref, gate_ref, o_ref):
    acc = jnp.dot(yp_ref[...], wp_ref[...], preferred_element_type=F32)
    acc += jnp.dot(yg_ref[...], wg_ref[...], preferred_element_type=F32)
    acc += jnp.dot(yr_ref[...], wr_ref[...], preferred_element_type=F32)
    o_ref[...] = x_ref[...] + gate_ref[...] * acc


def _outproj(yp, yg, yr, wp, wg, wr, x2, mod3, gate_blk0, tiles_per_batch, tm, tn):
    T, D = x2.shape
    return pl.pallas_call(
        _outproj_kernel,
        grid=(T // tm, D // tn),
        in_specs=[pl.BlockSpec((tm, yp.shape[1]), lambda i, j: (i, 0)),
                  pl.BlockSpec((tm, yg.shape[1]), lambda i, j: (i, 0)),
                  pl.BlockSpec((tm, yr.shape[1]), lambda i, j: (i, 0)),
                  pl.BlockSpec((wp.shape[0], tn), lambda i, j: (0, j)),
                  pl.BlockSpec((wg.shape[0], tn), lambda i, j: (0, j)),
                  pl.BlockSpec((wr.shape[0], tn), lambda i, j: (0, j)),
                  pl.BlockSpec((tm, tn), lambda i, j: (i, j)),
                  pl.BlockSpec((None, 1, tn), lambda i, j: (i // tiles_per_batch, 0, gate_blk0 + j))],
        out_specs=pl.BlockSpec((tm, tn), lambda i, j: (i, j)),
        out_shape=jax.ShapeDtypeStruct((T, D), F32),
        compiler_params=_cp(("parallel", "parallel")),
        name="out_projection",
    )(yp, yg, yr, wp, wg, wr, x2, mod3)


def _pool_kernel(p_ref, w_ref, scale_ref, o_ref, ext_ref, *, ts):
    i = pl.program_id(1)

    @pl.when(i == 0)
    def _():
        ext_ref[0:POOL_HALO, :] = jnp.zeros((POOL_HALO, POOL_WIDTH), F32)

    p = p_ref[...]
    ext_ref[POOL_HALO:POOL_HALO + ts, :] = p
    pos = (i * ts + _iota((ts, 1), 0) + 1).astype(F32)
    for g, win in enumerate(POOL_WINDOWS):
        lo = g * POOL_GROUP_DIM
        acc = p[:, lo:lo + POOL_GROUP_DIM]
        for j in range(1, win):
            acc = acc + ext_ref[POOL_HALO - j:POOL_HALO - j + ts, lo:lo + POOL_GROUP_DIM]
        d = acc / jnp.minimum(pos, float(win)) - p[:, lo:lo + POOL_GROUP_DIM]
        y = _dot(d, w_ref[g]) * scale_ref[:, lo:lo + POOL_GROUP_DIM]
        o_ref[:, lo:lo + POOL_GROUP_DIM] = y.astype(o_ref.dtype)
    ext_ref[0:POOL_HALO, :] = ext_ref[ts:ts + POOL_HALO, :]


def _pool(proj, pool_w, pool_scale, B, S, ts):
    T = B * S
    nt = S // ts
    return pl.pallas_call(
        functools.partial(_pool_kernel, ts=ts),
        grid=(B, nt),
        in_specs=[pl.BlockSpec((ts, POOL_WIDTH), lambda b, i: (b * nt + i, COL_POOL // POOL_WIDTH)),
                  pl.BlockSpec(pool_w.shape, lambda b, i: (0, 0, 0)),
                  pl.BlockSpec((1, POOL_WIDTH), lambda b, i: (0, 0))],
        out_specs=pl.BlockSpec((ts, POOL_WIDTH), lambda b, i: (b * nt + i, 0)),
        out_shape=jax.ShapeDtypeStruct((T, POOL_WIDTH), BF16),
        scratch_shapes=[pltpu.VMEM((ts + POOL_HALO, POOL_WIDTH), F32)],
        compiler_params=_cp(("arbitrary", "arbitrary")),
        name="causal_pool",
    )(proj, pool_w, pool_scale.reshape(1, POOL_WIDTH))


def _gla_kernel(q_ref, k_ref, v_ref, g_ref, ad_ref, up_ref, ab_ref, ng_ref, o_ref, st_ref):
    C, SUB = GLA_CHUNK, GLA_SUB

    @pl.when(pl.program_id(1) == 0)
    def _():
        st_ref[...] = jnp.zeros(st_ref.shape, F32)

    z = _dot(ad_ref[...], up_ref[...]) + ab_ref[...]
    la = -_softplus(-z) * (1.0 / GLA_TAU)
    bcum = _dot_r2(_tri_incl(C), la)
    rowc = _iota((C, 1), 0)
    causal = _iota((C, C), 1) <= _iota((C, C), 0)
    scale = GLA_DK ** -0.5
    H = range(GLA_HEADS)
    subs = range(C // SUB)
    ks = [slice(h * GLA_DK_PAD, (h + 1) * GLA_DK_PAD) for h in H]
    vs = [slice(h * GLA_DV_PAD, (h + 1) * GLA_DV_PAD) for h in H]
    q = [q_ref[:, ks[h]] * scale for h in H]
    k = [k_ref[:, ks[h]] for h in H]
    v = [v_ref[:, vs[h]] for h in H]
    b = [bcum[:, ks[h]] for h in H]
    st = [st_ref[h] for h in H]
    refs = [[b[h][s * SUB - 1:s * SUB, :] if s else jnp.zeros((1, GLA_DK_PAD), F32) for s in subs] for h in H]
    in_sub = [(rowc >= s * SUB) & (rowc < (s + 1) * SUB) for s in subs]
    q_bd = [jnp.concatenate([jnp.where(in_sub[s], q[h] * jnp.exp(b[h] - refs[h][s]), 0.0) for s in subs], axis=1)
            for h in H]
    k_cat = [jnp.concatenate([k[h] * jnp.exp(jnp.where(rowc < (s + 1) * SUB, refs[h][s] - b[h], 0.0)) for s in subs],
                             axis=1) for h in H]
    att = [jnp.where(causal, _dot(q_bd[h], k_cat[h], NT), 0.0) for h in H]
    out = [_dot(q[h] * jnp.exp(b[h]), st[h], NT) + _dot(att[h], v[h]) for h in H]
    b_last = [x[C - 1:C, :] for x in b]
    zpad_v = jnp.zeros((GLA_DV_PAD - C, GLA_DV_PAD), F32)
    zpad_k = jnp.zeros((GLA_DV_PAD - C, GLA_DK_PAD), F32)
    upd = [_dot(jnp.concatenate([v[h], zpad_v], axis=0).T,
                jnp.concatenate([k[h] * jnp.exp(b_last[h] - b[h]), zpad_k], axis=0)) for h in H]
    for h in H:
        st_ref[h] = st[h] * jnp.exp(b_last[h]) + upd[h]
    for h in H:
        ms = jnp.sum(out[h] * out[h], axis=-1, keepdims=True) * (1.0 / GLA_DV)
        o = out[h] * lax.rsqrt(ms + NORM_EPS) * ng_ref[:, vs[h]]
        gg = g_ref[:, vs[h]]
        o_ref[:, vs[h]] = (o * (gg * _sigmoid(gg))).astype(o_ref.dtype)


def _gla(proj, up_p, ab_p, ng_p, B, S):
    T = B * S
    C = GLA_CHUNK
    nc = S // C
    return pl.pallas_call(
        _gla_kernel,
        grid=(B, nc),
        in_specs=[pl.BlockSpec((C, GLA_QK_PAD), lambda b, i: (b * nc + i, COL_Q // GLA_QK_PAD)),
                  pl.BlockSpec((C, GLA_QK_PAD), lambda b, i: (b * nc + i, COL_K // GLA_QK_PAD)),
                  pl.BlockSpec((C, GLA_V_PAD), lambda b, i: (b * nc + i, COL_V // GLA_V_PAD)),
                  pl.BlockSpec((C, GLA_V_PAD), lambda b, i: (b * nc + i, COL_G // GLA_V_PAD)),
                  pl.BlockSpec((C, LANES), lambda b, i: (b * nc + i, COL_AD // LANES)),
                  pl.BlockSpec((LANES, GLA_QK_PAD), lambda b, i: (0, 0)),
                  pl.BlockSpec((1, GLA_QK_PAD), lambda b, i: (0, 0)),
                  pl.BlockSpec((1, GLA_V_PAD), lambda b, i: (0, 0))],
        out_specs=pl.BlockSpec((C, GLA_V_PAD), lambda b, i: (b * nc + i, 0)),
        out_shape=jax.ShapeDtypeStruct((T, GLA_V_PAD), BF16),
        scratch_shapes=[pltpu.VMEM((GLA_HEADS, GLA_DV_PAD, GLA_DK_PAD), F32)],
        compiler_params=_cp(("arbitrary", "arbitrary")),
        name="gla_chunked",
    )(proj, proj, proj, proj, proj, up_p, ab_p, ng_p)


def _rwkv_kernel(y_ref, mu_ref, w0_ref, w2_ref, a0_ref, a2_ref, g2_ref, kk_ref, ka_ref, rk_ref,
                 lg_ref, lb_ref, o_ref, st_ref, prev_ref):
    C = RWKV_CHUNK
    W = RWKV_WIDTH

    B = y_ref.shape[0]
    R = B * C

    @pl.when(pl.program_id(0) == 0)
    def _():
        st_ref[...] = jnp.zeros(st_ref.shape, F32)
        prev_ref[...] = jnp.zeros(prev_ref.shape, F32)

    y = jnp.concatenate([y_ref[b] for b in range(B)], axis=0)
    row = _iota((R, 1), 0)
    prev = prev_ref[B - 1:B, :]
    first = row == (B - 1) * C
    for b in range(B - 2, -1, -1):
        prev = jnp.where(row < (b + 1) * C, prev_ref[b:b + 1, :], prev)
        first = first | (row == b * C)
    shifted = jnp.where(first, prev, pltpu.roll(y, 1, 0))
    for b in range(B):
        prev_ref[b:b + 1, :] = y[(b + 1) * C - 1:(b + 1) * C, :]
    y = y + (shifted - y) * mu_ref[...]
    r_all = y[:, 0:W]
    k_all = y[:, W:2 * W]
    v_all = y[:, 2 * W:3 * W]
    wdad = y[:, 3 * W:3 * W + LANES]
    gd = y[:, 3 * W + LANES:3 * W + 2 * LANES]
    w_all = -_softplus(-(w0_ref[...] + _dot(jnp.tanh(wdad), w2_ref[...]))) - 0.5
    ld_all = -jnp.exp(w_all)
    a_all = _sigmoid(a0_ref[...] + _dot(wdad, a2_ref[...]))
    gate_all = _dot(_sigmoid(gd), g2_ref[...])

    lane = _iota((1, LANES), 1)
    m0 = (lane < RWKV_HEAD_DIM).astype(F32)
    m1 = 1.0 - m0
    seg = (_iota((LANES, LANES), 0) < RWKV_HEAD_DIM) == (_iota((LANES, LANES), 1) < RWKV_HEAD_DIM)
    seg_ones = seg.astype(BF16)
    ri = _iota((2 * C, 2 * C), 0)
    ci = _iota((2 * C, 2 * C), 1)
    same = (ri < C) == (ci < C)
    strict = same & (ci < ri)
    incl = same & (ci <= ri)
    eye = (ri == ci).astype(F32)
    rr, cc = _iota((R, R), 0), _iota((R, R), 1)
    shift = int(np.log2(C))
    tri = ((rr >= cc) & (lax.shift_right_logical(rr, shift) == lax.shift_right_logical(cc, shift))).astype(BF16)

    def stack(x):
        return jnp.concatenate([x * m0, x * m1], axis=0)

    ch = [(b, slice(b * C, (b + 1) * C), slice(p * LANES, (p + 1) * LANES))
          for b in range(B) for p in range(RWKV_PAIRS)]
    P = range(len(ch))
    r = [r_all[rs, s] for _, rs, s in ch]
    v = [v_all[rs, s] for _, rs, s in ch]
    a = [a_all[rs, s] for _, rs, s in ch]
    kk = [k_all[rs, s] * kk_ref[:, s] for _, rs, s in ch]
    nrm = [jnp.sqrt(_dot_l2(x * x, seg_ones)) for x in kk]
    kk = [kk[p] / jnp.maximum(nrm[p], 1e-12) for p in P]
    k = [k_all[ch[p][1], ch[p][2]] * (1.0 + (a[p] - 1.0) * ka_ref[:, ch[p][2]]) for p in P]
    beta = [kk[p] * a[p] for p in P]

    cum_all = _dot_r2(tri, ld_all)
    cum = [cum_all[rs, s] for _, rs, s in ch]
    c_last = [x[C - 1:C, :] for x in cum]
    at = [-kk[p] * jnp.exp(cum[p] - ld_all[ch[p][1], ch[p][2]]) for p in P]
    rt = [r[p] * jnp.exp(cum[p]) for p in P]
    pinv = [jnp.exp(-x) for x in cum]
    bt = [beta[p] * pinv[p] for p in P]
    kt = [k[p] * pinv[p] for p in P]
    dlast = [jnp.exp(c_last[p] - cum[p]) for p in P]
    ar2 = [jnp.concatenate([stack(at[p]), stack(rt[p])], axis=0) for p in P]
    bk2 = [jnp.concatenate([bt[p], bt[p], kt[p], kt[p]], axis=0) for p in P]
    v2 = [stack(x) for x in v]
    g = [st_ref[p] for p in P]

    sc = [_dot(ar2[p], bk2[p], NT) for p in P]
    a_ab = [jnp.where(strict, x[0:2 * C, 0:2 * C], 0.0) for x in sc]
    a_ak = [jnp.where(strict, x[0:2 * C, 2 * C:4 * C], 0.0) for x in sc]
    l_rbk = [jnp.concatenate([jnp.where(incl, x[2 * C:4 * C, 0:2 * C], 0.0),
                              jnp.where(incl, x[2 * C:4 * C, 2 * C:4 * C], 0.0)], axis=1) for x in sc]
    x0 = [_dot(ar2[p], g[p], NT) for p in P]
    rhs = [x0[p][0:2 * C, :] + _dot(a_ak[p], v2[p]) for p in P]
    t = [eye + x for x in a_ab]
    m = [_dot(x, x) for x in a_ab]
    for _ in range(int(np.log2(C)) - 2):
        tm = [_dot(jnp.concatenate([t[p], m[p]], axis=0), m[p]) for p in P]
        t = [t[p] + tm[p][0:2 * C, :] for p in P]
        m = [x[2 * C:4 * C, :] for x in tm]
    t = [t[p] + _dot(t[p], m[p]) for p in P]
    u2 = [_dot(t[p], rhs[p]) for p in P]
    uv = [jnp.concatenate([u2[p], v2[p]], axis=0) for p in P]
    o2 = [x0[p][2 * C:4 * C, :] + _dot(l_rbk[p], uv[p]) for p in P]
    o = [x[0:C, :] + x[C:2 * C, :] for x in o2]
    bkd = [jnp.concatenate([stack(beta[p] * dlast[p]), stack(k[p] * dlast[p])], axis=0) for p in P]
    upd = [_dot(jnp.concatenate([u2[p].T, v2[p].T], axis=1), bkd[p]) for p in P]
    for p in P:
        st_ref[p] = g[p] * jnp.exp(c_last[p]) + jnp.where(seg, upd[p], 0.0)

    mean = [_dot_l2(x, seg_ones) * (1.0 / RWKV_HEAD_DIM) for x in o]
    oc = [o[p] - mean[p] for p in P]
    var = [_dot_l2(x * x, seg_ones) * (1.0 / RWKV_HEAD_DIM) for x in oc]
    bonus = [_dot_l2(r[p] * k[p] * rk_ref[:, ch[p][2]], seg_ones) * v[p] for p in P]
    for p in P:
        b, rs, s = ch[p]
        on = oc[p] * lax.rsqrt(var[p] + RWKV_LNX_EPS) * lg_ref[:, s] + lb_ref[:, s]
        o_ref[b, :, s] = ((on + bonus[p]) * gate_all[rs, s]).astype(o_ref.dtype)


def _rwkv(proj, mu, w0, w2p, a0, a2p, g2, kk, ka, rk, lg, lb, B, S):
    T = B * S
    C = RWKV_CHUNK
    nc = S // C
    W = RWKV_WIDTH
    row = lambda n: pl.BlockSpec((1, n), lambda i: (0, 0))
    mat = lambda: pl.BlockSpec((LANES, W), lambda i: (0, 0))
    out = pl.pallas_call(
        _rwkv_kernel,
        grid=(nc,),
        in_specs=[pl.BlockSpec((B, C, RWKV_IN), lambda i: (0, i, COL_RWKV // RWKV_IN)),
                  row(RWKV_IN), row(W), mat(), row(W), mat(), mat(), row(W), row(W), row(W), row(W), row(W)],
        out_specs=pl.BlockSpec((B, C, W), lambda i: (0, i, 0)),
        out_shape=jax.ShapeDtypeStruct((B, S, W), BF16),
        scratch_shapes=[pltpu.VMEM((B * RWKV_PAIRS, LANES, LANES), F32),
                        pltpu.VMEM((B, RWKV_IN), F32)],
        compiler_params=_cp(("arbitrary",)),
        name="rwkv7_chunked",
    )(proj.reshape(B, S, proj.shape[1]), mu, w0, w2p, a0, a2p, g2, kk, ka, rk, lg, lb)
    return out.reshape(T, W)


def _take_top(works, n):
    works = list(works)
    rows = _iota(works[0].shape, 0)
    n_rows = works[0].shape[0]
    ranks = [jnp.full(w.shape, float(n), F32) for w in works]
    vals = [[] for _ in works]
    for i in range(n):
        m = [jnp.max(w, axis=0, keepdims=True) for w in works]
        first = [jnp.min(jnp.where(w == mm, rows, n_rows), axis=0, keepdims=True) for w, mm in zip(works, m)]
        hit = [rows == f for f in first]
        works = [jnp.where(h, -jnp.inf, w) for h, w in zip(hit, works)]
        ranks = [jnp.where(h, float(i), r) for h, r in zip(hit, ranks)]
        for v, mm in zip(vals, m):
            v.append(mm)
    return vals, ranks


def _batcher_network(n):
    pairs = []
    p = 1
    while p < n:
        k = p
        while k >= 1:
            for j in range(k % p, n - k, 2 * k):
                for i in range(min(k, n - j - k)):
                    if (i + j) // (2 * p) == (i + j + k) // (2 * p):
                        pairs.append((i + j, i + j + k))
            k //= 2
        p *= 2
    return pairs


def _sublane_all(x, op):
    for shift in (4, 2, 1):
        x = op(x, pltpu.roll(x, shift, 0))
    return x


def _sorted_top(pieces):
    p = list(pieces)
    n = len(p)
    for i, j in _batcher_network(n):
        p[i], p[j] = jnp.maximum(p[i], p[j]), jnp.minimum(p[i], p[j])
    for shift in (4, 2, 1):
        p = [jnp.maximum(p[i], pltpu.roll(p[n - 1 - i], shift, 0)) for i in range(n)]
        k = n // 2
        while k >= 1:
            for i in range(n):
                if not i & k:
                    p[i], p[i + k] = jnp.maximum(p[i], p[i + k]), jnp.minimum(p[i], p[i + k])
            k //= 2
    return p


def _peer_pairs():
    K = PEER_TOPK
    pairs = [(a, b) for a in range(K) for b in range(K) if (a + 1) * (b + 1) <= K]
    n_cand = len(pairs) + (-len(pairs) % 8)
    a_idx = _iota((K, n_cand), 0)
    c_idx = _iota((K, n_cand), 1)
    owner = jnp.zeros((K, n_cand), F32)
    for a in range(K):
        first = pairs.index((a, 0))
        owner = jnp.where((a_idx == a) & (c_idx >= first) & (c_idx < first + K // (a + 1)), 1.0, owner)
    return pairs, n_cand, owner


def _peer_score_kernel(q_ref, k1_ref, k2_ref, n1_ref, c1_ref, r2_ref, e2_ref):
    K = PEER_TOPK
    SUB = 8
    H = range(PEER_HEADS)
    lanes = q_ref.shape[0]
    n_pieces = PEER_NKEYS // SUB
    pairs, n_cand, owner = _peer_pairs()
    sub = _iota((SUB, lanes), 0)
    neg = jnp.full((SUB, lanes), -jnp.inf, F32)
    bad = jnp.zeros((SUB, lanes), F32)

    def scores(h):
        base = h * 2 * PEER_HALF
        return (_dot(k1_ref[h], q_ref[:, base:base + PEER_HALF], NT),
                _dot(k2_ref[h], q_ref[:, base + PEER_HALF:base + 2 * PEER_HALF], NT))

    def ambiguous(top, pieces):
        amb = jnp.zeros((SUB, lanes), F32)
        for a in range(len(top) - 1):
            amb = jnp.where(top[a] > top[a + 1], amb, 1.0)
        cnt = jnp.zeros((SUB, lanes), F32)
        for p in pieces:
            cnt = cnt + (p >= top[-1]).astype(F32)
        return jnp.where(_sublane_all(cnt, jnp.add) == len(top), amb, 1.0)

    for h in H:
        s1, s2 = scores(h)
        p1 = [s1[r * SUB:(r + 1) * SUB, :] for r in range(n_pieces)]
        p2 = [s2[r * SUB:(r + 1) * SUB, :] for r in range(n_pieces)]
        t1, t2 = _sorted_top(p1), _sorted_top(p2)
        bad = jnp.maximum(bad, jnp.maximum(ambiguous(t1, p1), ambiguous(t2, p2)))
        cand = []
        for k in range(n_cand // SUB):
            x = neg
            for j, (a, b) in enumerate(pairs[k * SUB:(k + 1) * SUB]):
                x = jnp.where(sub == j, t1[a] + t2[b], x)
            cand.append(x)
        best, work = [], list(cand)
        for _ in range(K):
            m = work[0]
            for w in work[1:]:
                m = jnp.maximum(m, w)
            m = _sublane_all(m, jnp.maximum)
            best.append(m)
            work = [jnp.where(w == m, neg, w) for w in work]
        bad = jnp.maximum(bad, ambiguous(best, cand))
        z = jnp.ones((SUB, lanes), F32)
        for c in best[1:]:
            z = z + jnp.exp(c - best[0])
        sel = jnp.concatenate([(c >= best[-1]).astype(F32) for c in cand], axis=0)
        n_sel = _dot(owner, sel)
        theta = jnp.full((K, lanes), -jnp.inf, F32)
        for b in range(K):
            theta = jnp.where(n_sel == b + 1, -jnp.concatenate([t2[b]] * (K // SUB), axis=0), theta)
        th = [jnp.broadcast_to(theta[a:a + 1, :], (SUB, lanes)) for a in range(K)]
        for r in range(n_pieces):
            rs = slice(r * SUB, (r + 1) * SUB)
            m1 = neg
            for a in range(K):
                m1 = jnp.where(p1[r] == t1[a], th[a], m1)
            n1_ref[h, rs, :] = m1
            c1_ref[h, rs, :] = jnp.exp(p1[r] - t1[0]) / z
            r2_ref[h, rs, :] = -p2[r]
            e2_ref[h, rs, :] = jnp.exp(p2[r] - t2[0])

    @pl.when(jnp.max(bad) > 0.0)
    def _():
        v1, rank1, cand = [], [], []
        for h in H:
            s1, s2 = scores(h)
            (va, vb), (ra, rb) = _take_top([s1, s2], K)
            pad = [jnp.full_like(va[0], -jnp.inf)] * (n_cand - len(pairs))
            cand.append(jnp.concatenate([va[a] + vb[b] for a, b in pairs] + pad, axis=0))
            v1.append(va[0])
            rank1.append(ra)
            c1_ref[h] = s1
            r2_ref[h] = rb
            e2_ref[h] = jnp.exp(s2 - vb[0])
        best, crank = _take_top(cand, K)
        for h in H:
            z = jnp.ones_like(best[h][0])
            for c in best[h][1:]:
                z = z + jnp.exp(c - best[h][0])
            n_sel = _dot(owner, (crank[h] < K).astype(F32))
            n1 = jnp.full_like(rank1[h], -1.0)
            for a in range(K):
                n1 = jnp.where(rank1[h] == a, n_sel[a:a + 1, :] - 1.0, n1)
            n1_ref[h] = n1
            c1_ref[h] = jnp.exp(c1_ref[h] - v1[h]) / z


def _peer_scores(q, k1, k2, tt):
    T = q.shape[0]
    H, NK = PEER_HEADS, PEER_NKEYS
    out = jax.ShapeDtypeStruct((H, NK, T), F32)
    ospec = pl.BlockSpec((H, NK, tt), lambda i: (0, 0, i))
    return pl.pallas_call(
        _peer_score_kernel,
        grid=(T // tt,),
        in_specs=[pl.BlockSpec((tt, q.shape[1]), lambda i: (i, 0)),
                  pl.BlockSpec(k1.shape, lambda i: (0, 0, 0)),
                  pl.BlockSpec(k2.shape, lambda i: (0, 0, 0))],
        out_specs=[ospec, ospec, ospec, ospec],
        out_shape=[out, out, out, out],
        compiler_params=_cp(("parallel",)),
        name="peer_scores",
    )(q, k1, k2)


def _gelu(x):
    return 0.5 * x * (1.0 + lax.erf(x * (2.0 ** -0.5)))


def _peer_dense_kernel(h_ref, u_ref, vt_ref, n1_ref, c1_ref, r2_ref, e2_ref,
                       o_ref, ht_ref, coeff_a_ref, coeff_b_ref, *group_refs, n_i1, blocks_per_tile):
    s = pl.program_id(0)
    NK = PEER_NKEYS
    n_groups = len(group_refs) // 2
    act_refs, acc_refs = group_refs[:n_groups], group_refs[n_groups:]
    rows_a = act_refs[0].shape[0]
    rows_c = acc_refs[0].shape[0]
    per_group = n_i1 // n_groups

    @pl.when(s == 0)
    def _():
        coeff_b_ref[...] = jnp.zeros(coeff_b_ref.shape, BF16)

    @pl.when(s % blocks_per_tile == 0)
    def _():
        ht_ref[...] = h_ref[...].astype(F32).T.astype(BF16)

    @pl.when((s == 0) | (s % blocks_per_tile == 1))
    def _():
        for acc_ref in acc_refs:
            acc_ref[...] = jnp.zeros(acc_ref.shape, F32)

    def activations(g):
        act_refs[g][...] = jnp.dot(u_ref[g * rows_a:(g + 1) * rows_a, :], ht_ref[...], preferred_element_type=F32)

    def accumulate(g, coeff_prev_ref):
        acc_refs[g][...] += jnp.dot(vt_ref[g * rows_c:(g + 1) * rows_c, :], coeff_prev_ref[...],
                                    preferred_element_type=F32)

    def coefficients(il, coeff_cur_ref):
        w = None
        for h in range(PEER_HEADS):
            n1 = n1_ref[h, il:il + 1, :]
            c1 = c1_ref[h, il:il + 1, :]
            term = jnp.where(r2_ref[h] <= n1, e2_ref[h] * c1, 0.0)
            w = term if w is None else w + term
        r0 = (il % per_group) * NK
        act = act_refs[il // per_group][r0:r0 + NK, :]
        coeff_cur_ref[il * NK:(il + 1) * NK, :] = (w * _gelu(act)).astype(BF16)

    def body(coeff_cur_ref, coeff_prev_ref):
        activations(0)
        for g in range(1, n_groups + 1):
            if g < n_groups:
                activations(g)
            accumulate(g - 1, coeff_prev_ref)
            for il in range((g - 1) * per_group, g * per_group):
                coefficients(il, coeff_cur_ref)

    @pl.when(s % 2 == 0)
    def _():
        body(coeff_a_ref, coeff_b_ref)

    @pl.when(s % 2 == 1)
    def _():
        body(coeff_b_ref, coeff_a_ref)

    @pl.when((s > 0) & (s % blocks_per_tile == 0))
    def _():
        for g, acc_ref in enumerate(acc_refs):
            o_ref[:, g * rows_c:(g + 1) * rows_c] = acc_ref[...].T


def _peer_dense(h, u, vt, n1, c1, r2, e2, tm, te):
    T, D = h.shape
    H, NK = PEER_HEADS, PEER_NKEYS
    n_i1 = te // NK
    bpt = PEER_N // te
    last = (T // tm) * bpt - 1
    cur = lambda s: jnp.minimum(s, last)
    prev = lambda s: jnp.maximum(s - 1, 0)
    row_spec = pl.BlockSpec((H, n_i1, tm), lambda s: (0, cur(s) % bpt, cur(s) // bpt))
    all_spec = pl.BlockSpec((H, NK, tm), lambda s: (0, 0, cur(s) // bpt))
    return pl.pallas_call(
        functools.partial(_peer_dense_kernel, n_i1=n_i1, blocks_per_tile=bpt),
        grid=(last + 2,),
        in_specs=[pl.BlockSpec((tm, D), lambda s: (cur(s) // bpt, 0)),
                  pl.BlockSpec((te, D), lambda s: (cur(s) % bpt, 0)),
                  pl.BlockSpec((D, te), lambda s: (0, prev(s) % bpt)),
                  row_spec, row_spec, all_spec, all_spec],
        out_specs=pl.BlockSpec((tm, D), lambda s: (prev(s) // bpt, 0)),
        out_shape=jax.ShapeDtypeStruct((T, D), F32),
        scratch_shapes=([pltpu.VMEM((D, tm), BF16), pltpu.VMEM((te, tm), BF16), pltpu.VMEM((te, tm), BF16)]
                        + [pltpu.VMEM((te // PEER_DENSE_GROUPS, tm), F32)] * PEER_DENSE_GROUPS
                        + [pltpu.VMEM((D // PEER_DENSE_GROUPS, tm), F32)] * PEER_DENSE_GROUPS),
        compiler_params=_cp(("arbitrary",)),
        name="peer_dense",
    )(h, u, vt, n1, c1, r2, e2)


def _pad_heads(w, heads, dim, dim_pad):
    lead = w.shape[:-1]
    w = w.reshape(lead + (heads, dim))
    w = jnp.pad(w, [(0, 0)] * len(lead) + [(0, 0), (0, dim_pad - dim)])
    return w.reshape(lead + (heads * dim_pad,))


def _layout_w_in(w_in):
    pool = w_in[..., 0:512]
    q = _pad_heads(w_in[..., 512:896], GLA_HEADS, GLA_DK, GLA_DK_PAD)
    k = _pad_heads(w_in[..., 896:1280], GLA_HEADS, GLA_DK, GLA_DK_PAD)
    v = _pad_heads(w_in[..., 1280:2048], GLA_HEADS, GLA_DV, GLA_DV_PAD)
    g = _pad_heads(w_in[..., 2048:2816], GLA_HEADS, GLA_DV, GLA_DV_PAD)
    ad = jnp.pad(w_in[..., 2816:2832], [(0, 0)] * (w_in.ndim - 1) + [(0, LANES - GLA_LORA)])
    rw = w_in[..., 2832:5392]
    return jnp.concatenate([rw, pool, q, k, v, g, ad], axis=-1).astype(BF16)


def kernel(x, c, ada_w, ada_b, norm1_g, w_in, pool_w, pool_scale, gla_alpha_up, gla_alpha_b, gla_norm_g, rwkv_mu, rwkv_w0, rwkv_w2, rwkv_a0, rwkv_a2, rwkv_g2, rwkv_kk, rwkv_ka, rwkv_rk, rwkv_lnx_g, rwkv_lnx_b, w_out, norm2_g, peer_wq, peer_k1, peer_k2, peer_u, peer_v, final_g):
    B, S, D = x.shape
    L = ada_w.shape[0]
    T = B * S
    tm = min(512, S)
    tpb = S // tm
    tm_mm = min(1024, S)
    tn_mm = 1024
    W = RWKV_WIDTH

    mod = _modulation(c, ada_w, ada_b)
    x2 = x.reshape(T, D)
    ffn = None
    for l in range(L):
        mod3 = mod[l].reshape(B, 1, 6 * D)
        if l == 0:
            h = _normmod(x2, norm1_g[l], mod3, 0, 1, tpb, tm)
        else:
            x2, h = _resid_normmod(x2, ffn, mod[l - 1].reshape(B, 1, 6 * D), 5, norm1_g[l], mod3, 0, 1, tpb, tm)
        proj = _matmul(h, _layout_w_in(w_in[l]), tm_mm, 896)
        y_pool = _pool(proj, pool_w[l], pool_scale[l], B, S, tm)
        up_p = jnp.pad(_pad_heads(gla_alpha_up[l], GLA_HEADS, GLA_DK, GLA_DK_PAD),
                       ((0, LANES - GLA_LORA), (0, 0)))
        ab_p = _pad_heads(gla_alpha_b[l], GLA_HEADS, GLA_DK, GLA_DK_PAD).reshape(1, -1)
        ng_p = _pad_heads(gla_norm_g[l], GLA_HEADS, GLA_DV, GLA_DV_PAD).reshape(1, -1)
        y_gla = _gla(proj, up_p, ab_p, ng_p, B, S)
        zeros = jnp.zeros((RWKV_HEAD_DIM, W), F32)
        w2p = jnp.concatenate([rwkv_w2[l], zeros], axis=0).astype(BF16)
        a2p = jnp.concatenate([zeros, rwkv_a2[l]], axis=0).astype(BF16)
        y_rwkv = _rwkv(proj, rwkv_mu[l].reshape(1, -1), rwkv_w0[l].reshape(1, W), w2p,
                       rwkv_a0[l].reshape(1, W), a2p, rwkv_g2[l].astype(BF16),
                       rwkv_kk[l].reshape(1, W), rwkv_ka[l].reshape(1, W), rwkv_rk[l].reshape(1, W),
                       rwkv_lnx_g[l].reshape(1, W), rwkv_lnx_b[l].reshape(1, W), B, S)
        wo = w_out[l]
        wo_g = jnp.pad(wo[512:1280].reshape(GLA_HEADS, GLA_DV, D),
                       ((0, 0), (0, GLA_DV_PAD - GLA_DV), (0, 0))).reshape(GLA_V_PAD, D)
        x2 = _outproj(y_pool, y_gla, y_rwkv, wo[0:512].astype(BF16), wo_g.astype(BF16),
                      wo[1280:].astype(BF16), x2, mod3, 2 * D // tn_mm, S // tm_mm, tm_mm, tn_mm)
        h = _normmod(x2, norm2_g[l], mod3, 3, 4, tpb, tm)
        q = _matmul(h, peer_wq[l].astype(BF16), tm_mm, tn_mm)
        n1, c1, r2, e2 = _peer_scores(q, peer_k1[l], peer_k2[l], 2 * LANES)
        ffn = _peer_dense(h, peer_u[l].astype(BF16), peer_v[l].T.astype(BF16), n1, c1, r2, e2, tm, 1024)
    out = _resid_final(x2, ffn, mod[L - 1].reshape(B, 1, 6 * D), 5, final_g, tpb, tm)
    return out.reshape(B, S, D)
```

```python
import functools

import jax
import jax.numpy as jnp
import numpy as np
from jax import lax
from jax.experimental import pallas as pl
from jax.experimental.pallas import tpu as pltpu

F32 = jnp.float32
BF16 = jnp.bfloat16

LANES = 128
NORM_EPS = 1e-6

POOL_WINDOWS = (2, 4, 8, 16)
POOL_GROUP_DIM = 128
POOL_WIDTH = 512
POOL_HALO = 16

GLA_HEADS = 4
GLA_DK = 96
GLA_DV = 192
GLA_DK_PAD = 128
GLA_DV_PAD = 256
GLA_LORA = 16
GLA_TAU = 16.0
GLA_CHUNK = 64
GLA_SUB = 16
GLA_QK_PAD = GLA_HEADS * GLA_DK_PAD
GLA_V_PAD = GLA_HEADS * GLA_DV_PAD

RWKV_WIDTH = 768
RWKV_HEAD_DIM = 64
RWKV_PAIRS = RWKV_WIDTH // LANES
RWKV_IN = 2560
RWKV_CHUNK = 64
RWKV_LNX_EPS = 64e-5

COL_RWKV = 0
COL_POOL = 2560
COL_Q = 3072
COL_K = 3584
COL_V = 4096
COL_G = 5120
COL_AD = 6144
IN_PAD = 6272

PEER_HEADS = 8
PEER_NKEYS = 128
PEER_HALF = 128
PEER_TOPK = 16
PEER_N = PEER_NKEYS * PEER_NKEYS
PEER_DENSE_GROUPS = 4

NN = (((1,), (0,)), ((), ()))
NT = (((1,), (1,)), ((), ()))

VMEM_LIMIT = 56 * 1024 * 1024


def _cp(sem):
    return pltpu.CompilerParams(dimension_semantics=sem, vmem_limit_bytes=VMEM_LIMIT)


def _dot(a, b, dims=NN):
    return lax.dot_general(a.astype(BF16), b.astype(BF16), dims, preferred_element_type=F32)


def _split(x):
    hi = x.astype(BF16)
    lo = (x - hi.astype(F32)).astype(BF16)
    return hi, lo


def _dot_l2(a, b, dims=NN):
    hi, lo = _split(a)
    bb = b.astype(BF16)
    return (lax.dot_general(hi, bb, dims, preferred_element_type=F32)
            + lax.dot_general(lo, bb, dims, preferred_element_type=F32))


def _dot_r2(a, b):
    hi, lo = _split(b)
    ab = a.astype(BF16)
    return (lax.dot_general(ab, hi, NN, preferred_element_type=F32)
            + lax.dot_general(ab, lo, NN, preferred_element_type=F32))


def _iota(shape, dim):
    return lax.broadcasted_iota(jnp.int32, shape, dim)


def _tri_incl(n):
    return (_iota((n, n), 0) >= _iota((n, n), 1)).astype(BF16)


def _sigmoid(x):
    return 1.0 / (1.0 + jnp.exp(-x))


def _softplus(x):
    return jnp.maximum(x, 0.0) + jnp.log1p(jnp.exp(-jnp.abs(x)))


def _mod_kernel(c_ref, w_ref, b_ref, o_ref):
    c = c_ref[...]
    ca = c * _sigmoid(c)
    o_ref[...] = jnp.dot(ca, w_ref[...], precision=lax.Precision.HIGHEST,
                         preferred_element_type=F32) + b_ref[...]


def _modulation(c, ada_w, ada_b):
    L, D, N6 = ada_w.shape
    B = c.shape[0]
    tn = 1024
    return pl.pallas_call(
        _mod_kernel,
        grid=(L, N6 // tn),
        in_specs=[pl.BlockSpec((B, D), lambda l, j: (0, 0)),
                  pl.BlockSpec((None, D, tn), lambda l, j: (l, 0, j)),
                  pl.BlockSpec((None, 1, tn), lambda l, j: (l, 0, j))],
        out_specs=pl.BlockSpec((None, B, tn), lambda l, j: (l, 0, j)),
        out_shape=jax.ShapeDtypeStruct((L, B, N6), F32),
        compiler_params=_cp(("parallel", "parallel")),
        name="adaln_mod",
    )(c, ada_w, ada_b.reshape(L, 1, N6))


def _normmod_kernel(x_ref, g_ref, sh_ref, sc_ref, o_ref):
    x = x_ref[...]
    y = x * lax.rsqrt(jnp.mean(x * x, axis=-1, keepdims=True) + NORM_EPS) * g_ref[...]
    o_ref[...] = (y * (1.0 + sc_ref[...]) + sh_ref[...]).astype(o_ref.dtype)


def _normmod(x2, g, mod3, sh_blk, sc_blk, tiles_per_batch, tm):
    T, D = x2.shape
    return pl.pallas_call(
        _normmod_kernel,
        grid=(T // tm,),
        in_specs=[pl.BlockSpec((tm, D), lambda i: (i, 0)),
                  pl.BlockSpec((1, D), lambda i: (0, 0)),
                  pl.BlockSpec((None, 1, D), lambda i: (i // tiles_per_batch, 0, sh_blk)),
                  pl.BlockSpec((None, 1, D), lambda i: (i // tiles_per_batch, 0, sc_blk))],
        out_specs=pl.BlockSpec((tm, D), lambda i: (i, 0)),
        out_shape=jax.ShapeDtypeStruct((T, D), BF16),
        compiler_params=_cp(("parallel",)),
        name="rmsnorm_adaln",
    )(x2, g.reshape(1, D), mod3, mod3)


def _resid_normmod_kernel(x_ref, f_ref, gate_ref, g_ref, sh_ref, sc_ref, xo_ref, h_ref):
    x = x_ref[...] + gate_ref[...] * f_ref[...]
    xo_ref[...] = x
    y = x * lax.rsqrt(jnp.mean(x * x, axis=-1, keepdims=True) + NORM_EPS) * g_ref[...]
    h_ref[...] = (y * (1.0 + sc_ref[...]) + sh_ref[...]).astype(h_ref.dtype)


def _resid_normmod(x2, f, mod_prev3, gate_blk, g, mod3, sh_blk, sc_blk, tiles_per_batch, tm):
    T, D = x2.shape
    row = lambda blk: pl.BlockSpec((None, 1, D), lambda i: (i // tiles_per_batch, 0, blk))
    tile = pl.BlockSpec((tm, D), lambda i: (i, 0))
    return pl.pallas_call(
        _resid_normmod_kernel,
        grid=(T // tm,),
        in_specs=[tile, tile, row(gate_blk), pl.BlockSpec((1, D), lambda i: (0, 0)), row(sh_blk), row(sc_blk)],
        out_specs=[tile, tile],
        out_shape=[jax.ShapeDtypeStruct((T, D), F32), jax.ShapeDtypeStruct((T, D), BF16)],
        compiler_params=_cp(("parallel",)),
        name="residual_rmsnorm_adaln",
    )(x2, f, mod_prev3, g.reshape(1, D), mod3, mod3)


def _resid_final_kernel(x_ref, f_ref, gate_ref, g_ref, o_ref):
    x = x_ref[...] + gate_ref[...] * f_ref[...]
    o_ref[...] = x * lax.rsqrt(jnp.mean(x * x, axis=-1, keepdims=True) + NORM_EPS) * g_ref[...]


def _resid_final(x2, f, mod_prev3, gate_blk, g, tiles_per_batch, tm):
    T, D = x2.shape
    tile = pl.BlockSpec((tm, D), lambda i: (i, 0))
    return pl.pallas_call(
        _resid_final_kernel,
        grid=(T // tm,),
        in_specs=[tile, tile,
                  pl.BlockSpec((None, 1, D), lambda i: (i // tiles_per_batch, 0, gate_blk)),
                  pl.BlockSpec((1, D), lambda i: (0, 0))],
        out_specs=tile,
        out_shape=jax.ShapeDtypeStruct((T, D), F32),
        compiler_params=_cp(("parallel",)),
        name="residual_final_rmsnorm",
    )(x2, f, mod_prev3, g.reshape(1, D))


def _matmul_kernel(a_ref, w_ref, o_ref):
    o_ref[...] = jnp.dot(a_ref[...], w_ref[...], preferred_element_type=F32)


def _matmul(a, w, tm, tn):
    M, K = a.shape
    N = w.shape[1]
    return pl.pallas_call(
        _matmul_kernel,
        grid=(M // tm, N // tn),
        in_specs=[pl.BlockSpec((tm, K), lambda i, j: (i, 0)),
                  pl.BlockSpec((K, tn), lambda i, j: (0, j))],
        out_specs=pl.BlockSpec((tm, tn), lambda i, j: (i, j)),
        out_shape=jax.ShapeDtypeStruct((M, N), F32),
        compiler_params=_cp(("parallel", "parallel")),
        name="projection",
    )(a, w)


def _outproj_kernel(yp_ref, yg_ref, yr_ref, wp_ref, wg_ref, wr_ref, x_ref, gate_ref, o_ref):
    acc = jnp.dot(yp_ref[...], wp_ref[...], preferred_element_type=F32)
    acc += jnp.dot(yg_ref[...], wg_ref[...], preferred_element_type=F32)
    acc += jnp.dot(yr_ref[...], wr_ref[...], preferred_element_type=F32)
    o_ref[...] = x_ref[...] + gate_ref[...] * acc


def _outproj(yp, yg, yr, wp, wg, wr, x2, mod3, gate_blk0, tiles_per_batch, tm, tn):
    T, D = x2.shape
    return pl.pallas_call(
        _outproj_kernel,
        grid=(T // tm, D // tn),
        in_specs=[pl.BlockSpec((tm, yp.shape[1]), lambda i, j: (i, 0)),
                  pl.BlockSpec((tm, yg.shape[1]), lambda i, j: (i, 0)),
                  pl.BlockSpec((tm, yr.shape[1]), lambda i, j: (i, 0)),
                  pl.BlockSpec((wp.shape[0], tn), lambda i, j: (0, j)),
                  pl.BlockSpec((wg.shape[0], tn), lambda i, j: (0, j)),
                  pl.BlockSpec((wr.shape[0], tn), lambda i, j: (0, j)),
                  pl.BlockSpec((tm, tn), lambda i, j: (i, j)),
                  pl.BlockSpec((None, 1, tn), lambda i, j: (i // tiles_per_batch, 0, gate_blk0 + j))],
        out_specs=pl.BlockSpec((tm, tn), lambda i, j: (i, j)),
        out_shape=jax.ShapeDtypeStruct((T, D), F32),
        compiler_params=_cp(("parallel", "parallel")),
        name="out_projection",
    )(yp, yg, yr, wp, wg, wr, x2, mod3)


def _pool_kernel(p_ref, w_ref, scale_ref, o_ref, ext_ref, *, ts):
    i = pl.program_id(1)

    @pl.when(i == 0)
    def _():
        ext_ref[0:POOL_HALO, :] = jnp.zeros((POOL_HALO, POOL_WIDTH), F32)

    p = p_ref[...]
    ext_ref[POOL_HALO:POOL_HALO + ts, :] = p
    pos = (i * ts + _iota((ts, 1), 0) + 1).astype(F32)
    for g, win in enumerate(POOL_WINDOWS):
        lo = g * POOL_GROUP_DIM
        acc = p[:, lo:lo + POOL_GROUP_DIM]
        for j in range(1, win):
            acc = acc + ext_ref[POOL_HALO - j:POOL_HALO - j + ts, lo:lo + POOL_GROUP_DIM]
        d = acc / jnp.minimum(pos, float(win)) - p[:, lo:lo + POOL_GROUP_DIM]
        y = _dot(d, w_ref[g]) * scale_ref[:, lo:lo + POOL_GROUP_DIM]
        o_ref[:, lo:lo + POOL_GROUP_DIM] = y.astype(o_ref.dtype)
    ext_ref[0:POOL_HALO, :] = ext_ref[ts:ts + POOL_HALO, :]


def _pool(proj, pool_w, pool_scale, B, S, ts):
    T = B * S
    nt = S // ts
    return pl.pallas_call(
        functools.partial(_pool_kernel, ts=ts),
        grid=(B, nt),
        in_specs=[pl.BlockSpec((ts, POOL_WIDTH), lambda b, i: (b * nt + i, COL_POOL // POOL_WIDTH)),
                  pl.BlockSpec(pool_w.shape, lambda b, i: (0, 0, 0)),
                  pl.BlockSpec((1, POOL_WIDTH), lambda b, i: (0, 0))],
        out_specs=pl.BlockSpec((ts, POOL_WIDTH), lambda b, i: (b * nt + i, 0)),
        out_shape=jax.ShapeDtypeStruct((T, POOL_WIDTH), BF16),
        scratch_shapes=[pltpu.VMEM((ts + POOL_HALO, POOL_WIDTH), F32)],
        compiler_params=_cp(("arbitrary", "arbitrary")),
        name="causal_pool",
    )(proj, pool_w, pool_scale.reshape(1, POOL_WIDTH))


def _gla_kernel(q_ref, k_ref, v_ref, g_ref, ad_ref, up_ref, ab_ref, ng_ref, o_ref, st_ref):
    C, SUB = GLA_CHUNK, GLA_SUB

    @pl.when(pl.program_id(1) == 0)
    def _():
        st_ref[...] = jnp.zeros(st_ref.shape, F32)

    z = _dot(ad_ref[...], up_ref[...]) + ab_ref[...]
    la = -_softplus(-z) * (1.0 / GLA_TAU)
    bcum = _dot_r2(_tri_incl(C), la)
    rowc = _iota((C, 1), 0)
    causal = _iota((C, C), 1) <= _iota((C, C), 0)
    scale = GLA_DK ** -0.5
    H = range(GLA_HEADS)
    subs = range(C // SUB)
    ks = [slice(h * GLA_DK_PAD, (h + 1) * GLA_DK_PAD) for h in H]
    vs = [slice(h * GLA_DV_PAD, (h + 1) * GLA_DV_PAD) for h in H]
    q = [q_ref[:, ks[h]] * scale for h in H]
    k = [k_ref[:, ks[h]] for h in H]
    v = [v_ref[:, vs[h]] for h in H]
    b = [bcum[:, ks[h]] for h in H]
    st = [st_ref[h] for h in H]
    refs = [[b[h][s * SUB - 1:s * SUB, :] if s else jnp.zeros((1, GLA_DK_PAD), F32) for s in subs] for h in H]
    in_sub = [(rowc >= s * SUB) & (rowc < (s + 1) * SUB) for s in subs]
    q_bd = [jnp.concatenate([jnp.where(in_sub[s], q[h] * jnp.exp(b[h] - refs[h][s]), 0.0) for s in subs], axis=1)
            for h in H]
    k_cat = [jnp.concatenate([k[h] * jnp.exp(jnp.where(rowc < (s + 1) * SUB, refs[h][s] - b[h], 0.0)) for s in subs],
                             axis=1) for h in H]
    att = [jnp.where(causal, _dot(q_bd[h], k_cat[h], NT), 0.0) for h in H]
    out = [_dot(q[h] * jnp.exp(b[h]), st[h], NT) + _dot(att[h], v[h]) for h in H]
    b_last = [x[C - 1:C, :] for x in b]
    zpad_v = jnp.zeros((GLA_DV_PAD - C, GLA_DV_PAD), F32)
    zpad_k = jnp.zeros((GLA_DV_PAD - C, GLA_DK_PAD), F32)
    upd = [_dot(jnp.concatenate([v[h], zpad_v], axis=0).T,
                jnp.concatenate([k[h] * jnp.exp(b_last[h] - b[h]), zpad_k], axis=0)) for h in H]
    for h in H:
        st_ref[h] = st[h] * jnp.exp(b_last[h]) + upd[h]
    for h in H:
        ms = jnp.sum(out[h] * out[h], axis=-1, keepdims=True) * (1.0 / GLA_DV)
        o = out[h] * lax.rsqrt(ms + NORM_EPS) * ng_ref[:, vs[h]]
        gg = g_ref[:, vs[h]]
        o_ref[:, vs[h]] = (o * (gg * _sigmoid(gg))).astype(o_ref.dtype)


def _gla(proj, up_p, ab_p, ng_p, B, S):
    T = B * S
    C = GLA_CHUNK
    nc = S // C
    return pl.pallas_call(
        _gla_kernel,
        grid=(B, nc),
        in_specs=[pl.BlockSpec((C, GLA_QK_PAD), lambda b, i: (b * nc + i, COL_Q // GLA_QK_PAD)),
                  pl.BlockSpec((C, GLA_QK_PAD), lambda b, i: (b * nc + i, COL_K // GLA_QK_PAD)),
                  pl.BlockSpec((C, GLA_V_PAD), lambda b, i: (b * nc + i, COL_V // GLA_V_PAD)),
                  pl.BlockSpec((C, GLA_V_PAD), lambda b, i: (b * nc + i, COL_G // GLA_V_PAD)),
                  pl.BlockSpec((C, LANES), lambda b, i: (b * nc + i, COL_AD // LANES)),
                  pl.BlockSpec((LANES, GLA_QK_PAD), lambda b, i: (0, 0)),
                  pl.BlockSpec((1, GLA_QK_PAD), lambda b, i: (0, 0)),
                  pl.BlockSpec((1, GLA_V_PAD), lambda b, i: (0, 0))],
        out_specs=pl.BlockSpec((C, GLA_V_PAD), lambda b, i: (b * nc + i, 0)),
        out_shape=jax.ShapeDtypeStruct((T, GLA_V_PAD), BF16),
        scratch_shapes=[pltpu.VMEM((GLA_HEADS, GLA_DV_PAD, GLA_DK_PAD), F32)],
        compiler_params=_cp(("arbitrary", "arbitrary")),
        name="gla_chunked",
    )(proj, proj, proj, proj, proj, up_p, ab_p, ng_p)


def _rwkv_kernel(y_ref, mu_ref, w0_ref, w2_ref, a0_ref, a2_ref, g2_ref, kk_ref, ka_ref, rk_ref,
                 lg_ref, lb_ref, o_ref, st_ref, prev_ref):
    C = RWKV_CHUNK
    W = RWKV_WIDTH

    B = y_ref.shape[0]
    R = B * C

    @pl.when(pl.program_id(0) == 0)
    def _():
        st_ref[...] = jnp.zeros(st_ref.shape, F32)
        prev_ref[...] = jnp.zeros(prev_ref.shape, F32)

    y = jnp.concatenate([y_ref[b] for b in range(B)], axis=0)
    row = _iota((R, 1), 0)
    prev = prev_ref[B - 1:B, :]
    first = row == (B - 1) * C
    for b in range(B - 2, -1, -1):
        prev = jnp.where(row < (b + 1) * C, prev_ref[b:b + 1, :], prev)
        first = first | (row == b * C)
    shifted = jnp.where(first, prev, pltpu.roll(y, 1, 0))
    for b in range(B):
        prev_ref[b:b + 1, :] = y[(b + 1) * C - 1:(b + 1) * C, :]
    y = y + (shifted - y) * mu_ref[...]
    r_all = y[:, 0:W]
    k_all = y[:, W:2 * W]
    v_all = y[:, 2 * W:3 * W]
    wdad = y[:, 3 * W:3 * W + LANES]
    gd = y[:, 3 * W + LANES:3 * W + 2 * LANES]
    w_all = -_softplus(-(w0_ref[...] + _dot(jnp.tanh(wdad), w2_ref[...]))) - 0.5
    ld_all = -jnp.exp(w_all)
    a_all = _sigmoid(a0_ref[...] + _dot(wdad, a2_ref[...]))
    gate_all = _dot(_sigmoid(gd), g2_ref[...])

    lane = _iota((1, LANES), 1)
    m0 = (lane < RWKV_HEAD_DIM).astype(F32)
    m1 = 1.0 - m0
    seg = (_iota((LANES, LANES), 0) < RWKV_HEAD_DIM) == (_iota((LANES, LANES), 1) < RWKV_HEAD_DIM)
    seg_ones = seg.astype(BF16)
    ri = _iota((2 * C, 2 * C), 0)
    ci = _iota((2 * C, 2 * C), 1)
    same = (ri < C) == (ci < C)
    strict = same & (ci < ri)
    incl = same & (ci <= ri)
    eye = (ri == ci).astype(F32)
    rr, cc = _iota((R, R), 0), _iota((R, R), 1)
    shift = int(np.log2(C))
    tri = ((rr >= cc) & (lax.shift_right_logical(rr, shift) == lax.shift_right_logical(cc, shift))).astype(BF16)

    def stack(x):
        return jnp.concatenate([x * m0, x * m1], axis=0)

    ch = [(b, slice(b * C, (b + 1) * C), slice(p * LANES, (p + 1) * LANES))
          for b in range(B) for p in range(RWKV_PAIRS)]
    P = range(len(ch))
    r = [r_all[rs, s] for _, rs, s in ch]
    v = [v_all[rs, s] for _, rs, s in ch]
    a = [a_all[rs, s] for _, rs, s in ch]
    kk = [k_all[rs, s] * kk_ref[:, s] for _, rs, s in ch]
    nrm = [jnp.sqrt(_dot_l2(x * x, seg_ones)) for x in kk]
    kk = [kk[p] / jnp.maximum(nrm[p], 1e-12) for p in P]
    k = [k_all[ch[p][1], ch[p][2]] * (1.0 + (a[p] - 1.0) * ka_ref[:, ch[p][2]]) for p in P]
    beta = [kk[p] * a[p] for p in P]

    cum_all = _dot_r2(tri, ld_all)
    cum = [cum_all[rs, s] for _, rs, s in ch]
    c_last = [x[C - 1:C, :] for x in cum]
    at = [-kk[p] * jnp.exp(cum[p] - ld_all[ch[p][1], ch[p][2]]) for p in P]
    rt = [r[p] * jnp.exp(cum[p]) for p in P]
    pinv = [jnp.exp(-x) for x in cum]
    bt = [beta[p] * pinv[p] for p in P]
    kt = [k[p] * pinv[p] for p in P]
    dlast = [jnp.exp(c_last[p] - cum[p]) for p in P]
    ar2 = [jnp.concatenate([stack(at[p]), stack(rt[p])], axis=0) for p in P]
    bk2 = [jnp.concatenate([bt[p], bt[p], kt[p], kt[p]], axis=0) for p in P]
    v2 = [stack(x) for x in v]
    g = [st_ref[p] for p in P]

    sc = [_dot(ar2[p], bk2[p], NT) for p in P]
    a_ab = [jnp.where(strict, x[0:2 * C, 0:2 * C], 0.0) for x in sc]
    a_ak = [jnp.where(strict, x[0:2 * C, 2 * C:4 * C], 0.0) for x in sc]
    l_rbk = [jnp.concatenate([jnp.where(incl, x[2 * C:4 * C, 0:2 * C], 0.0),
                              jnp.where(incl, x[2 * C:4 * C, 2 * C:4 * C], 0.0)], axis=1) for x in sc]
    x0 = [_dot(ar2[p], g[p], NT) for p in P]
    rhs = [x0[p][0:2 * C, :] + _dot(a_ak[p], v2[p]) for p in P]
    t = [eye + x for x in a_ab]
    m = [_dot(x, x) for x in a_ab]
    for _ in range(int(np.log2(C)) - 2):
        tm = [_dot(jnp.concatenate([t[p], m[p]], axis=0), m[p]) for p in P]
        t = [t[p] + tm[p][0:2 * C, :] for p in P]
        m = [x[2 * C:4 * C, :] for x in tm]
    t = [t[p] + _dot(t[p], m[p]) for p in P]
    u2 = [_dot(t[p], rhs[p]) for p in P]
    uv = [jnp.concatenate([u2[p], v2[p]], axis=0) for p in P]
    o2 = [x0[p][2 * C:4 * C, :] + _dot(l_rbk[p], uv[p]) for p in P]
    o = [x[0:C, :] + x[C:2 * C, :] for x in o2]
    bkd = [jnp.concatenate([stack(beta[p] * dlast[p]), stack(k[p] * dlast[p])], axis=0) for p in P]
    upd = [_dot(jnp.concatenate([u2[p].T, v2[p].T], axis=1), bkd[p]) for p in P]
    for p in P:
        st_ref[p] = g[p] * jnp.exp(c_last[p]) + jnp.where(seg, upd[p], 0.0)

    mean = [_dot_l2(x, seg_ones) * (1.0 / RWKV_HEAD_DIM) for x in o]
    oc = [o[p] - mean[p] for p in P]
    var = [_dot_l2(x * x, seg_ones) * (1.0 / RWKV_HEAD_DIM) for x in oc]
    bonus = [_dot_l2(r[p] * k[p] * rk_ref[:, ch[p][2]], seg_ones) * v[p] for p in P]
    for p in P:
        b, rs, s = ch[p]
        on = oc[p] * lax.rsqrt(var[p] + RWKV_LNX_EPS) * lg_ref[:, s] + lb_ref[:, s]
        o_ref[b, :, s] = ((on + bonus[p]) * gate_all[rs, s]).astype(o_ref.dtype)


def _rwkv(proj, mu, w0, w2p, a0, a2p, g2, kk, ka, rk, lg, lb, B, S):
    T = B * S
    C = RWKV_CHUNK
    nc = S // C
    W = RWKV_WIDTH
    row = lambda n: pl.BlockSpec((1, n), lambda i: (0, 0))
    mat = lambda: pl.BlockSpec((LANES, W), lambda i: (0, 0))
    out = pl.pallas_call(
        _rwkv_kernel,
        grid=(nc,),
        in_specs=[pl.BlockSpec((B, C, RWKV_IN), lambda i: (0, i, COL_RWKV // RWKV_IN)),
                  row(RWKV_IN), row(W), mat(), row(W), mat(), mat(), row(W), row(W), row(W), row(W), row(W)],
        out_specs=pl.BlockSpec((B, C, W), lambda i: (0, i, 0)),
        out_shape=jax.ShapeDtypeStruct((B, S, W), BF16),
        scratch_shapes=[pltpu.VMEM((B * RWKV_PAIRS, LANES, LANES), F32),
                        pltpu.VMEM((B, RWKV_IN), F32)],
        compiler_params=_cp(("arbitrary",)),
        name="rwkv7_chunked",
    )(proj.reshape(B, S, proj.shape[1]), mu, w0, w2p, a0, a2p, g2, kk, ka, rk, lg, lb)
    return out.reshape(T, W)


def _take_top(works, n):
    works = list(works)
    rows = _iota(works[0].shape, 0)
    n_rows = works[0].shape[0]
    ranks = [jnp.full(w.shape, float(n), F32) for w in works]
    vals = [[] for _ in works]
    for i in range(n):
        m = [jnp.max(w, axis=0, keepdims=True) for w in works]
        first = [jnp.min(jnp.where(w == mm, rows, n_rows), axis=0, keepdims=True) for w, mm in zip(works, m)]
        hit = [rows == f for f in first]
        works = [jnp.where(h, -jnp.inf, w) for h, w in zip(hit, works)]
        ranks = [jnp.where(h, float(i), r) for h, r in zip(hit, ranks)]
        for v, mm in zip(vals, m):
            v.append(mm)
    return vals, ranks


def _batcher_network(n):
    pairs = []
    p = 1
    while p < n:
        k = p
        while k >= 1:
            for j in range(k % p, n - k, 2 * k):
                for i in range(min(k, n - j - k)):
                    if (i + j) // (2 * p) == (i + j + k) // (2 * p):
                        pairs.append((i + j, i + j + k))
            k //= 2
        p *= 2
    return pairs


def _sublane_all(x, op):
    for shift in (4, 2, 1):
        x = op(x, pltpu.roll(x, shift, 0))
    return x


def _sorted_top(pieces):
    p = list(pieces)
    n = len(p)
    for i, j in _batcher_network(n):
        p[i], p[j] = jnp.maximum(p[i], p[j]), jnp.minimum(p[i], p[j])
    for shift in (4, 2, 1):
        p = [jnp.maximum(p[i], pltpu.roll(p[n - 1 - i], shift, 0)) for i in range(n)]
        k = n // 2
        while k >= 1:
            for i in range(n):
                if not i & k:
                    p[i], p[i + k] = jnp.maximum(p[i], p[i + k]), jnp.minimum(p[i], p[i + k])
            k //= 2
    return p


def _peer_pairs():
    K = PEER_TOPK
    pairs = [(a, b) for a in range(K) for b in range(K) if (a + 1) * (b + 1) <= K]
    n_cand = len(pairs) + (-len(pairs) % 8)
    a_idx = _iota((K, n_cand), 0)
    c_idx = _iota((K, n_cand), 1)
    owner = jnp.zeros((K, n_cand), F32)
    for a in range(K):
        first = pairs.index((a, 0))
        owner = jnp.where((a_idx == a) & (c_idx >= first) & (c_idx < first + K // (a + 1)), 1.0, owner)
    return pairs, n_cand, owner


def _peer_score_kernel(q_ref, k1_ref, k2_ref, n1_ref, c1_ref, r2_ref, e2_ref):
    K = PEER_TOPK
    SUB = 8
    H = range(PEER_HEADS)
    lanes = q_ref.shape[0]
    n_pieces = PEER_NKEYS // SUB
    pairs, n_cand, owner = _peer_pairs()
    sub = _iota((SUB, lanes), 0)
    neg = jnp.full((SUB, lanes), -jnp.inf, F32)
    bads = []

    def scores(h):
        base = h * 2 * PEER_HALF
        return (_dot(k1_ref[h], q_ref[:, base:base + PEER_HALF], NT),
                _dot(k2_ref[h], q_ref[:, base + PEER_HALF:base + 2 * PEER_HALF], NT))

    def ambiguous(top, pieces):
        amb = jnp.zeros((SUB, lanes), F32)
        for a in range(len(top) - 1):
            amb = jnp.where(top[a] > top[a + 1], amb, 1.0)
        cnt = jnp.zeros((SUB, lanes), F32)
        for p in pieces:
            cnt = cnt + (p >= top[-1]).astype(F32)
        return jnp.where(_sublane_all(cnt, jnp.add) == len(top), amb, 1.0)

    for h in H:
        s1, s2 = scores(h)
        p1 = [s1[r * SUB:(r + 1) * SUB, :] for r in range(n_pieces)]
        p2 = [s2[r * SUB:(r + 1) * SUB, :] for r in range(n_pieces)]
        t1, t2 = _sorted_top(p1), _sorted_top(p2)
        bad = jnp.maximum(ambiguous(t1, p1), ambiguous(t2, p2))
        cand = []
        for k in range(n_cand // SUB):
            x = neg
            for j, (a, b) in enumerate(pairs[k * SUB:(k + 1) * SUB]):
                x = jnp.where(sub == j, t1[a] + t2[b], x)
            cand.append(x)
        best, work = [], list(cand)
        for _ in range(K):
            m = work[0]
            for w in work[1:]:
                m = jnp.maximum(m, w)
            m = _sublane_all(m, jnp.maximum)
            best.append(m)
            work = [jnp.where(w == m, neg, w) for w in work]
        bads.append(jnp.maximum(bad, ambiguous(best, cand)))
        z = jnp.ones((SUB, lanes), F32)
        for c in best[1:]:
            z = z + jnp.exp(c - best[0])
        sel = jnp.concatenate([(c >= best[-1]).astype(F32) for c in cand], axis=0)
        n_sel = _dot(owner, sel)
        theta = jnp.full((K, lanes), -jnp.inf, F32)
        for b in range(K):
            theta = jnp.where(n_sel == b + 1, -jnp.concatenate([t2[b]] * (K // SUB), axis=0), theta)
        th = [jnp.broadcast_to(theta[a:a + 1, :], (SUB, lanes)) for a in range(K)]
        for r in range(n_pieces):
            rs = slice(r * SUB, (r + 1) * SUB)
            m1 = neg
            for a in range(K):
                m1 = jnp.where(p1[r] == t1[a], th[a], m1)
            n1_ref[h, rs, :] = m1
            c1_ref[h, rs, :] = jnp.exp(p1[r] - t1[0]) / z
            r2_ref[h, rs, :] = -p2[r]
            e2_ref[h, rs, :] = jnp.exp(p2[r] - t2[0])

    def exact_order(h):
        s1, s2 = scores(h)
        (va, vb), (ra, rb) = _take_top([s1, s2], K)
        pad = [jnp.full_like(va[0], -jnp.inf)] * (n_cand - len(pairs))
        (best,), (crank,) = _take_top([jnp.concatenate([va[a] + vb[b] for a, b in pairs] + pad, axis=0)], K)
        z = jnp.ones_like(best[0])
        for c in best[1:]:
            z = z + jnp.exp(c - best[0])
        n_sel = _dot(owner, (crank < K).astype(F32))
        n1 = jnp.full_like(ra, -1.0)
        for a in range(K):
            n1 = jnp.where(ra == a, n_sel[a:a + 1, :] - 1.0, n1)
        n1_ref[h] = n1
        c1_ref[h] = jnp.exp(s1 - va[0]) / z
        r2_ref[h] = rb
        e2_ref[h] = jnp.exp(s2 - vb[0])

    any_bad = functools.reduce(jnp.maximum, bads)

    @pl.when(jnp.max(any_bad) > 0.0)
    def _():
        for h in H:
            pl.when(jnp.max(bads[h]) > 0.0)(functools.partial(exact_order, h))


def _peer_scores(q, k1, k2, tt):
    T = q.shape[0]
    H, NK = PEER_HEADS, PEER_NKEYS
    out = jax.ShapeDtypeStruct((H, NK, T), F32)
    ospec = pl.BlockSpec((H, NK, tt), lambda i: (0, 0, i))
    return pl.pallas_call(
        _peer_score_kernel,
        grid=(T // tt,),
        in_specs=[pl.BlockSpec((tt, q.shape[1]), lambda i: (i, 0)),
                  pl.BlockSpec(k1.shape, lambda i: (0, 0, 0)),
                  pl.BlockSpec(k2.shape, lambda i: (0, 0, 0))],
        out_specs=[ospec, ospec, ospec, ospec],
        out_shape=[out, out, out, out],
        compiler_params=_cp(("parallel",)),
        name="peer_scores",
    )(q, k1, k2)


def _gelu(x):
    return 0.5 * x * (1.0 + lax.erf(x * (2.0 ** -0.5)))


def _peer_dense_kernel(h_ref, u_ref, vt_ref, n1_ref, c1_ref, r2_ref, e2_ref,
                       o_ref, ht_ref, coeff_a_ref, coeff_b_ref, *group_refs, n_i1, blocks_per_tile):
    s = pl.program_id(0)
    NK = PEER_NKEYS
    n_groups = len(group_refs) // 2
    act_refs, acc_refs = group_refs[:n_groups], group_refs[n_groups:]
    rows_a = act_refs[0].shape[0]
    rows_c = acc_refs[0].shape[0]
    per_group = n_i1 // n_groups

    @pl.when(s == 0)
    def _():
        coeff_b_ref[...] = jnp.zeros(coeff_b_ref.shape, BF16)

    @pl.when(s % blocks_per_tile == 0)
    def _():
        ht_ref[...] = h_ref[...].astype(F32).T.astype(BF16)

    @pl.when((s == 0) | (s % blocks_per_tile == 1))
    def _():
        for acc_ref in acc_refs:
            acc_ref[...] = jnp.zeros(acc_ref.shape, F32)

    def activations(g):
        act_refs[g][...] = jnp.dot(u_ref[g * rows_a:(g + 1) * rows_a, :], ht_ref[...], preferred_element_type=F32)

    def accumulate(g, coeff_prev_ref):
        acc_refs[g][...] += jnp.dot(vt_ref[g * rows_c:(g + 1) * rows_c, :], coeff_prev_ref[...],
                                    preferred_element_type=F32)

    def coefficients(il, coeff_cur_ref):
        w = None
        for h in range(PEER_HEADS):
            n1 = n1_ref[h, il:il + 1, :]
            c1 = c1_ref[h, il:il + 1, :]
            term = jnp.where(r2_ref[h] <= n1, e2_ref[h] * c1, 0.0)
            w = term if w is None else w + term
        r0 = (il % per_group) * NK
        act = act_refs[il // per_group][r0:r0 + NK, :]
        coeff_cur_ref[il * NK:(il + 1) * NK, :] = (w * _gelu(act)).astype(BF16)

    def body(coeff_cur_ref, coeff_prev_ref):
        activations(0)
        for g in range(1, n_groups + 1):
            if g < n_groups:
                activations(g)
            accumulate(g - 1, coeff_prev_ref)
            for il in range((g - 1) * per_group, g * per_group):
                coefficients(il, coeff_cur_ref)

    @pl.when(s % 2 == 0)
    def _():
        body(coeff_a_ref, coeff_b_ref)

    @pl.when(s % 2 == 1)
    def _():
        body(coeff_b_ref, coeff_a_ref)

    @pl.when((s > 0) & (s % blocks_per_tile == 0))
    def _():
        for g, acc_ref in enumerate(acc_refs):
            o_ref[:, g * rows_c:(g + 1) * rows_c] = acc_ref[...].T


def _peer_dense(h, u, vt, n1, c1, r2, e2, tm, te):
    T, D = h.shape
    H, NK = PEER_HEADS, PEER_NKEYS
    n_i1 = te // NK
    bpt = PEER_N // te
    last = (T // tm) * bpt - 1
    cur = lambda s: jnp.minimum(s, last)
    prev = lambda s: jnp.maximum(s - 1, 0)
    row_spec = pl.BlockSpec((H, n_i1, tm), lambda s: (0, cur(s) % bpt, cur(s) // bpt))
    all_spec = pl.BlockSpec((H, NK, tm), lambda s: (0, 0, cur(s) // bpt))
    return pl.pallas_call(
        functools.partial(_peer_dense_kernel, n_i1=n_i1, blocks_per_tile=bpt),
        grid=(last + 2,),
        in_specs=[pl.BlockSpec((tm, D), lambda s: (cur(s) // bpt, 0)),
                  pl.BlockSpec((te, D), lambda s: (cur(s) % bpt, 0)),
                  pl.BlockSpec((D, te), lambda s: (0, prev(s) % bpt)),
                  row_spec, row_spec, all_spec, all_spec],
        out_specs=pl.BlockSpec((tm, D), lambda s: (prev(s) // bpt, 0)),
        out_shape=jax.ShapeDtypeStruct((T, D), F32),
        scratch_shapes=([pltpu.VMEM((D, tm), BF16), pltpu.VMEM((te, tm), BF16), pltpu.VMEM((te, tm), BF16)]
                        + [pltpu.VMEM((te // PEER_DENSE_GROUPS, tm), F32)] * PEER_DENSE_GROUPS
                        + [pltpu.VMEM((D // PEER_DENSE_GROUPS, tm), F32)] * PEER_DENSE_GROUPS),
        compiler_params=_cp(("arbitrary",)),
        name="peer_dense",
    )(h, u, vt, n1, c1, r2, e2)


def _pad_heads(w, heads, dim, dim_pad):
    lead = w.shape[:-1]
    w = w.reshape(lead + (heads, dim))
    w = jnp.pad(w, [(0, 0)] * len(lead) + [(0, 0), (0, dim_pad - dim)])
    return w.reshape(lead + (heads * dim_pad,))


def _layout_w_in(w_in):
    pool = w_in[..., 0:512]
    q = _pad_heads(w_in[..., 512:896], GLA_HEADS, GLA_DK, GLA_DK_PAD)
    k = _pad_heads(w_in[..., 896:1280], GLA_HEADS, GLA_DK, GLA_DK_PAD)
    v = _pad_heads(w_in[..., 1280:2048], GLA_HEADS, GLA_DV, GLA_DV_PAD)
    g = _pad_heads(w_in[..., 2048:2816], GLA_HEADS, GLA_DV, GLA_DV_PAD)
    ad = jnp.pad(w_in[..., 2816:2832], [(0, 0)] * (w_in.ndim - 1) + [(0, LANES - GLA_LORA)])
    rw = w_in[..., 2832:5392]
    return jnp.concatenate([rw, pool, q, k, v, g, ad], axis=-1).astype(BF16)


def kernel(x, c, ada_w, ada_b, norm1_g, w_in, pool_w, pool_scale, gla_alpha_up, gla_alpha_b, gla_norm_g, rwkv_mu, rwkv_w0, rwkv_w2, rwkv_a0, rwkv_a2, rwkv_g2, rwkv_kk, rwkv_ka, rwkv_rk, rwkv_lnx_g, rwkv_lnx_b, w_out, norm2_g, peer_wq, peer_k1, peer_k2, peer_u, peer_v, final_g):
    B, S, D = x.shape
    L = ada_w.shape[0]
    T = B * S
    tm = min(512, S)
    tpb = S // tm
    tm_mm = min(1024, S)
    tn_mm = 1024
    W = RWKV_WIDTH

    mod = _modulation(c, ada_w, ada_b)
    x2 = x.reshape(T, D)
    ffn = None
    for l in range(L):
        mod3 = mod[l].reshape(B, 1, 6 * D)
        if l == 0:
            h = _normmod(x2, norm1_g[l], mod3, 0, 1, tpb, tm)
        else:
            x2, h = _resid_normmod(x2, ffn, mod[l - 1].reshape(B, 1, 6 * D), 5, norm1_g[l], mod3, 0, 1, tpb, tm)
        proj = _matmul(h, _layout_w_in(w_in[l]), tm_mm, 896)
        y_pool = _pool(proj, pool_w[l], pool_scale[l], B, S, tm)
        up_p = jnp.pad(_pad_heads(gla_alpha_up[l], GLA_HEADS, GLA_DK, GLA_DK_PAD),
                       ((0, LANES - GLA_LORA), (0, 0)))
        ab_p = _pad_heads(gla_alpha_b[l], GLA_HEADS, GLA_DK, GLA_DK_PAD).reshape(1, -1)
        ng_p = _pad_heads(gla_norm_g[l], GLA_HEADS, GLA_DV, GLA_DV_PAD).reshape(1, -1)
        y_gla = _gla(proj, up_p, ab_p, ng_p, B, S)
        zeros = jnp.zeros((RWKV_HEAD_DIM, W), F32)
        w2p = jnp.concatenate([rwkv_w2[l], zeros], axis=0).astype(BF16)
        a2p = jnp.concatenate([zeros, rwkv_a2[l]], axis=0).astype(BF16)
        y_rwkv = _rwkv(proj, rwkv_mu[l].reshape(1, -1), rwkv_w0[l].reshape(1, W), w2p,
                       rwkv_a0[l].reshape(1, W), a2p, rwkv_g2[l].astype(BF16),
                       rwkv_kk[l].reshape(1, W), rwkv_ka[l].reshape(1, W), rwkv_rk[l].reshape(1, W),
                       rwkv_lnx_g[l].reshape(1, W), rwkv_lnx_b[l].reshape(1, W), B, S)
        wo = w_out[l]
        wo_g = jnp.pad(wo[512:1280].reshape(GLA_HEADS, GLA_DV, D),
                       ((0, 0), (0, GLA_DV_PAD - GLA_DV), (0, 0))).reshape(GLA_V_PAD, D)
        x2 = _outproj(y_pool, y_gla, y_rwkv, wo[0:512].astype(BF16), wo_g.astype(BF16),
                      wo[1280:].astype(BF16), x2, mod3, 2 * D // tn_mm, S // tm_mm, tm_mm, tn_mm)
        h = _normmod(x2, norm2_g[l], mod3, 3, 4, tpb, tm)
        q = _matmul(h, peer_wq[l].astype(BF16), tm_mm, tn_mm)
        n1, c1, r2, e2 = _peer_scores(q, peer_k1[l], peer_k2[l], 2 * LANES)
        ffn = _peer_dense(h, peer_u[l].astype(BF16), peer_v[l].T.astype(BF16), n1, c1, r2, e2, tm, 1024)
    out = _resid_final(x2, ffn, mod[L - 1].reshape(B, 1, 6 * D), 5, final_g, tpb, tm)
    return out.reshape(B, S, D)
```

```python
import functools

import jax
import jax.numpy as jnp
import numpy as np
from jax import lax
from jax.experimental import pallas as pl
from jax.experimental.pallas import tpu as pltpu

F32 = jnp.float32
BF16 = jnp.bfloat16

LANES = 128
NORM_EPS = 1e-6

POOL_WINDOWS = (2, 4, 8, 16)
POOL_GROUP_DIM = 128
POOL_WIDTH = 512
POOL_HALO = 16

GLA_HEADS = 4
GLA_DK = 96
GLA_DV = 192
GLA_DK_PAD = 128
GLA_DV_PAD = 256
GLA_LORA = 16
GLA_TAU = 16.0
GLA_CHUNK = 64
GLA_SUB = 16
GLA_QK_PAD = GLA_HEADS * GLA_DK_PAD
GLA_V_PAD = GLA_HEADS * GLA_DV_PAD

RWKV_WIDTH = 768
RWKV_HEAD_DIM = 64
RWKV_PAIRS = RWKV_WIDTH // LANES
RWKV_IN = 2560
RWKV_CHUNK = 64
RWKV_LNX_EPS = 64e-5

COL_RWKV = 0
COL_POOL = 2560
COL_Q = 3072
COL_K = 3584
COL_V = 4096
COL_G = 5120
COL_AD = 6144
IN_PAD = 6272

PEER_HEADS = 8
PEER_NKEYS = 128
PEER_HALF = 128
PEER_TOPK = 16
PEER_N = PEER_NKEYS * PEER_NKEYS
PEER_DENSE_GROUPS = 8

NN = (((1,), (0,)), ((), ()))
NT = (((1,), (1,)), ((), ()))

VMEM_LIMIT = 56 * 1024 * 1024


def _cp(sem):
    return pltpu.CompilerParams(dimension_semantics=sem, vmem_limit_bytes=VMEM_LIMIT)


def _dot(a, b, dims=NN):
    return lax.dot_general(a.astype(BF16), b.astype(BF16), dims, preferred_element_type=F32)


def _split(x):
    hi = x.astype(BF16)
    lo = (x - hi.astype(F32)).astype(BF16)
    return hi, lo


def _dot_l2(a, b, dims=NN):
    hi, lo = _split(a)
    bb = b.astype(BF16)
    return (lax.dot_general(hi, bb, dims, preferred_element_type=F32)
            + lax.dot_general(lo, bb, dims, preferred_element_type=F32))


def _dot_r2(a, b):
    hi, lo = _split(b)
    ab = a.astype(BF16)
    return (lax.dot_general(ab, hi, NN, preferred_element_type=F32)
            + lax.dot_general(ab, lo, NN, preferred_element_type=F32))


def _iota(shape, dim):
    return lax.broadcasted_iota(jnp.int32, shape, dim)


def _tri_incl(n):
    return (_iota((n, n), 0) >= _iota((n, n), 1)).astype(BF16)


def _sigmoid(x):
    return 1.0 / (1.0 + jnp.exp(-x))


def _softplus(x):
    return jnp.maximum(x, 0.0) + jnp.log1p(jnp.exp(-jnp.abs(x)))


def _mod_kernel(c_ref, w_ref, b_ref, o_ref):
    c = c_ref[...]
    ca = c * _sigmoid(c)
    o_ref[...] = jnp.dot(ca, w_ref[...], precision=lax.Precision.HIGHEST,
                         preferred_element_type=F32) + b_ref[...]


def _modulation(c, ada_w, ada_b):
    L, D, N6 = ada_w.shape
    B = c.shape[0]
    tn = 1024
    return pl.pallas_call(
        _mod_kernel,
        grid=(L, N6 // tn),
        in_specs=[pl.BlockSpec((B, D), lambda l, j: (0, 0)),
                  pl.BlockSpec((None, D, tn), lambda l, j: (l, 0, j)),
                  pl.BlockSpec((None, 1, tn), lambda l, j: (l, 0, j))],
        out_specs=pl.BlockSpec((None, B, tn), lambda l, j: (l, 0, j)),
        out_shape=jax.ShapeDtypeStruct((L, B, N6), F32),
        compiler_params=_cp(("parallel", "parallel")),
        name="adaln_mod",
    )(c, ada_w, ada_b.reshape(L, 1, N6))


def _normmod_kernel(x_ref, g_ref, sh_ref, sc_ref, o_ref):
    x = x_ref[...]
    y = x * lax.rsqrt(jnp.mean(x * x, axis=-1, keepdims=True) + NORM_EPS) * g_ref[...]
    o_ref[...] = (y * (1.0 + sc_ref[...]) + sh_ref[...]).astype(o_ref.dtype)


def _normmod(x2, g, mod3, sh_blk, sc_blk, tiles_per_batch, tm):
    T, D = x2.shape
    return pl.pallas_call(
        _normmod_kernel,
        grid=(T // tm,),
        in_specs=[pl.BlockSpec((tm, D), lambda i: (i, 0)),
                  pl.BlockSpec((1, D), lambda i: (0, 0)),
                  pl.BlockSpec((None, 1, D), lambda i: (i // tiles_per_batch, 0, sh_blk)),
                  pl.BlockSpec((None, 1, D), lambda i: (i // tiles_per_batch, 0, sc_blk))],
        out_specs=pl.BlockSpec((tm, D), lambda i: (i, 0)),
        out_shape=jax.ShapeDtypeStruct((T, D), BF16),
        compiler_params=_cp(("parallel",)),
        name="rmsnorm_adaln",
    )(x2, g.reshape(1, D), mod3, mod3)


def _resid_normmod_kernel(x_ref, f_ref, gate_ref, g_ref, sh_ref, sc_ref, xo_ref, h_ref):
    x = x_ref[...] + gate_ref[...] * f_ref[...]
    xo_ref[...] = x
    y = x * lax.rsqrt(jnp.mean(x * x, axis=-1, keepdims=True) + NORM_EPS) * g_ref[...]
    h_ref[...] = (y * (1.0 + sc_ref[...]) + sh_ref[...]).astype(h_ref.dtype)


def _resid_normmod(x2, f, mod_prev3, gate_blk, g, mod3, sh_blk, sc_blk, tiles_per_batch, tm):
    T, D = x2.shape
    row = lambda blk: pl.BlockSpec((None, 1, D), lambda i: (i // tiles_per_batch, 0, blk))
    tile = pl.BlockSpec((tm, D), lambda i: (i, 0))
    return pl.pallas_call(
        _resid_normmod_kernel,
        grid=(T // tm,),
        in_specs=[tile, tile, row(gate_blk), pl.BlockSpec((1, D), lambda i: (0, 0)), row(sh_blk), row(sc_blk)],
        out_specs=[tile, tile],
        out_shape=[jax.ShapeDtypeStruct((T, D), F32), jax.ShapeDtypeStruct((T, D), BF16)],
        compiler_params=_cp(("parallel",)),
        name="residual_rmsnorm_adaln",
    )(x2, f, mod_prev3, g.reshape(1, D), mod3, mod3)


def _resid_final_kernel(x_ref, f_ref, gate_ref, g_ref, o_ref):
    x = x_ref[...] + gate_ref[...] * f_ref[...]
    o_ref[...] = x * lax.rsqrt(jnp.mean(x * x, axis=-1, keepdims=True) + NORM_EPS) * g_ref[...]


def _resid_final(x2, f, mod_prev3, gate_blk, g, tiles_per_batch, tm):
    T, D = x2.shape
    tile = pl.BlockSpec((tm, D), lambda i: (i, 0))
    return pl.pallas_call(
        _resid_final_kernel,
        grid=(T // tm,),
        in_specs=[tile, tile,
                  pl.BlockSpec((None, 1, D), lambda i: (i // tiles_per_batch, 0, gate_blk)),
                  pl.BlockSpec((1, D), lambda i: (0, 0))],
        out_specs=tile,
        out_shape=jax.ShapeDtypeStruct((T, D), F32),
        compiler_params=_cp(("parallel",)),
        name="residual_final_rmsnorm",
    )(x2, f, mod_prev3, g.reshape(1, D))


def _matmul_kernel(a_ref, w_ref, o_ref):
    o_ref[...] = jnp.dot(a_ref[...], w_ref[...], preferred_element_type=F32)


def _matmul(a, w, tm, tn):
    M, K = a.shape
    N = w.shape[1]
    return pl.pallas_call(
        _matmul_kernel,
        grid=(M // tm, N // tn),
        in_specs=[pl.BlockSpec((tm, K), lambda i, j: (i, 0)),
                  pl.BlockSpec((K, tn), lambda i, j: (0, j))],
        out_specs=pl.BlockSpec((tm, tn), lambda i, j: (i, j)),
        out_shape=jax.ShapeDtypeStruct((M, N), F32),
        compiler_params=_cp(("parallel", "parallel")),
        name="projection",
    )(a, w)


def _outproj_kernel(yp_ref, yg_ref, yr_ref, wp_ref, wg_ref, wr_ref, x_ref, gate_ref, o_ref):
    acc = jnp.dot(yp_ref[...], wp_ref[...], preferred_element_type=F32)
    acc += jnp.dot(yg_ref[...], wg_ref[...], preferred_element_type=F32)
    acc += jnp.dot(yr_ref[...], wr_ref[...], preferred_element_type=F32)
    o_ref[...] = x_ref[...] + gate_ref[...] * acc


def _outproj(yp, yg, yr, wp, wg, wr, x2, mod3, gate_blk0, tiles_per_batch, tm, tn):
    T, D = x2.shape
    return pl.pallas_call(
        _outproj_kernel,
        grid=(T // tm, D // tn),
        in_specs=[pl.BlockSpec((tm, yp.shape[1]), lambda i, j: (i, 0)),
                  pl.BlockSpec((tm, yg.shape[1]), lambda i, j: (i, 0)),
                  pl.BlockSpec((tm, yr.shape[1]), lambda i, j: (i, 0)),
                  pl.BlockSpec((wp.shape[0], tn), lambda i, j: (0, j)),
                  pl.BlockSpec((wg.shape[0], tn), lambda i, j: (0, j)),
                  pl.BlockSpec((wr.shape[0], tn), lambda i, j: (0, j)),
                  pl.BlockSpec((tm, tn), lambda i, j: (i, j)),
                  pl.BlockSpec((None, 1, tn), lambda i, j: (i // tiles_per_batch, 0, gate_blk0 + j))],
        out_specs=pl.BlockSpec((tm, tn), lambda i, j: (i, j)),
        out_shape=jax.ShapeDtypeStruct((T, D), F32),
        compiler_params=_cp(("parallel", "parallel")),
        name="out_projection",
    )(yp, yg, yr, wp, wg, wr, x2, mod3)


def _pool_kernel(p_ref, w_ref, scale_ref, o_ref, ext_ref, *, ts):
    i = pl.program_id(1)

    @pl.when(i == 0)
    def _():
        ext_ref[0:POOL_HALO, :] = jnp.zeros((POOL_HALO, POOL_WIDTH), F32)

    p = p_ref[...]
    ext_ref[POOL_HALO:POOL_HALO + ts, :] = p
    pos = (i * ts + _iota((ts, 1), 0) + 1).astype(F32)
    for g, win in enumerate(POOL_WINDOWS):
        lo = g * POOL_GROUP_DIM
        acc = p[:, lo:lo + POOL_GROUP_DIM]
        for j in range(1, win):
            acc = acc + ext_ref[POOL_HALO - j:POOL_HALO - j + ts, lo:lo + POOL_GROUP_DIM]
        d = acc / jnp.minimum(pos, float(win)) - p[:, lo:lo + POOL_GROUP_DIM]
        y = _dot(d, w_ref[g]) * scale_ref[:, lo:lo + POOL_GROUP_DIM]
        o_ref[:, lo:lo + POOL_GROUP_DIM] = y.astype(o_ref.dtype)
    ext_ref[0:POOL_HALO, :] = ext_ref[ts:ts + POOL_HALO, :]


def _pool(proj, pool_w, pool_scale, B, S, ts):
    T = B * S
    nt = S // ts
    return pl.pallas_call(
        functools.partial(_pool_kernel, ts=ts),
        grid=(B, nt),
        in_specs=[pl.BlockSpec((ts, POOL_WIDTH), lambda b, i: (b * nt + i, COL_POOL // POOL_WIDTH)),
                  pl.BlockSpec(pool_w.shape, lambda b, i: (0, 0, 0)),
                  pl.BlockSpec((1, POOL_WIDTH), lambda b, i: (0, 0))],
        out_specs=pl.BlockSpec((ts, POOL_WIDTH), lambda b, i: (b * nt + i, 0)),
        out_shape=jax.ShapeDtypeStruct((T, POOL_WIDTH), BF16),
        scratch_shapes=[pltpu.VMEM((ts + POOL_HALO, POOL_WIDTH), F32)],
        compiler_params=_cp(("arbitrary", "arbitrary")),
        name="causal_pool",
    )(proj, pool_w, pool_scale.reshape(1, POOL_WIDTH))


def _gla_kernel(q_ref, k_ref, v_ref, g_ref, ad_ref, up_ref, ab_ref, ng_ref, o_ref, st_ref):
    C, SUB = GLA_CHUNK, GLA_SUB

    @pl.when(pl.program_id(1) == 0)
    def _():
        st_ref[...] = jnp.zeros(st_ref.shape, F32)

    z = _dot(ad_ref[...], up_ref[...]) + ab_ref[...]
    la = -_softplus(-z) * (1.0 / GLA_TAU)
    bcum = _dot_r2(_tri_incl(C), la)
    rowc = _iota((C, 1), 0)
    causal = _iota((C, C), 1) <= _iota((C, C), 0)
    scale = GLA_DK ** -0.5
    H = range(GLA_HEADS)
    subs = range(C // SUB)
    ks = [slice(h * GLA_DK_PAD, (h + 1) * GLA_DK_PAD) for h in H]
    vs = [slice(h * GLA_DV_PAD, (h + 1) * GLA_DV_PAD) for h in H]
    q = [q_ref[:, ks[h]] * scale for h in H]
    k = [k_ref[:, ks[h]] for h in H]
    v = [v_ref[:, vs[h]] for h in H]
    b = [bcum[:, ks[h]] for h in H]
    st = [st_ref[h] for h in H]
    refs = [[b[h][s * SUB - 1:s * SUB, :] if s else jnp.zeros((1, GLA_DK_PAD), F32) for s in subs] for h in H]
    in_sub = [(rowc >= s * SUB) & (rowc < (s + 1) * SUB) for s in subs]
    q_bd = [jnp.concatenate([jnp.where(in_sub[s], q[h] * jnp.exp(b[h] - refs[h][s]), 0.0) for s in subs], axis=1)
            for h in H]
    k_cat = [jnp.concatenate([k[h] * jnp.exp(jnp.where(rowc < (s + 1) * SUB, refs[h][s] - b[h], 0.0)) for s in subs],
                             axis=1) for h in H]
    att = [jnp.where(causal, _dot(q_bd[h], k_cat[h], NT), 0.0) for h in H]
    out = [_dot(q[h] * jnp.exp(b[h]), st[h], NT) + _dot(att[h], v[h]) for h in H]
    b_last = [x[C - 1:C, :] for x in b]
    zpad_v = jnp.zeros((GLA_DV_PAD - C, GLA_DV_PAD), F32)
    zpad_k = jnp.zeros((GLA_DV_PAD - C, GLA_DK_PAD), F32)
    upd = [_dot(jnp.concatenate([v[h], zpad_v], axis=0).T,
                jnp.concatenate([k[h] * jnp.exp(b_last[h] - b[h]), zpad_k], axis=0)) for h in H]
    for h in H:
        st_ref[h] = st[h] * jnp.exp(b_last[h]) + upd[h]
    for h in H:
        ms = jnp.sum(out[h] * out[h], axis=-1, keepdims=True) * (1.0 / GLA_DV)
        o = out[h] * lax.rsqrt(ms + NORM_EPS) * ng_ref[:, vs[h]]
        gg = g_ref[:, vs[h]]
        o_ref[:, vs[h]] = (o * (gg * _sigmoid(gg))).astype(o_ref.dtype)


def _gla(proj, up_p, ab_p, ng_p, B, S):
    T = B * S
    C = GLA_CHUNK
    nc = S // C
    return pl.pallas_call(
        _gla_kernel,
        grid=(B, nc),
        in_specs=[pl.BlockSpec((C, GLA_QK_PAD), lambda b, i: (b * nc + i, COL_Q // GLA_QK_PAD)),
                  pl.BlockSpec((C, GLA_QK_PAD), lambda b, i: (b * nc + i, COL_K // GLA_QK_PAD)),
                  pl.BlockSpec((C, GLA_V_PAD), lambda b, i: (b * nc + i, COL_V // GLA_V_PAD)),
                  pl.BlockSpec((C, GLA_V_PAD), lambda b, i: (b * nc + i, COL_G // GLA_V_PAD)),
                  pl.BlockSpec((C, LANES), lambda b, i: (b * nc + i, COL_AD // LANES)),
                  pl.BlockSpec((LANES, GLA_QK_PAD), lambda b, i: (0, 0)),
                  pl.BlockSpec((1, GLA_QK_PAD), lambda b, i: (0, 0)),
                  pl.BlockSpec((1, GLA_V_PAD), lambda b, i: (0, 0))],
        out_specs=pl.BlockSpec((C, GLA_V_PAD), lambda b, i: (b * nc + i, 0)),
        out_shape=jax.ShapeDtypeStruct((T, GLA_V_PAD), BF16),
        scratch_shapes=[pltpu.VMEM((GLA_HEADS, GLA_DV_PAD, GLA_DK_PAD), F32)],
        compiler_params=_cp(("arbitrary", "arbitrary")),
        name="gla_chunked",
    )(proj, proj, proj, proj, proj, up_p, ab_p, ng_p)


def _rwkv_kernel(y_ref, mu_ref, w0_ref, w2_ref, a0_ref, a2_ref, g2_ref, kk_ref, ka_ref, rk_ref,
                 lg_ref, lb_ref, o_ref, st_ref, prev_ref):
    C = RWKV_CHUNK
    W = RWKV_WIDTH

    B = y_ref.shape[0]
    R = B * C

    @pl.when(pl.program_id(0) == 0)
    def _():
        st_ref[...] = jnp.zeros(st_ref.shape, F32)
        prev_ref[...] = jnp.zeros(prev_ref.shape, F32)

    y = jnp.concatenate([y_ref[b] for b in range(B)], axis=0)
    row = _iota((R, 1), 0)
    prev = prev_ref[B - 1:B, :]
    first = row == (B - 1) * C
    for b in range(B - 2, -1, -1):
        prev = jnp.where(row < (b + 1) * C, prev_ref[b:b + 1, :], prev)
        first = first | (row == b * C)
    shifted = jnp.where(first, prev, pltpu.roll(y, 1, 0))
    for b in range(B):
        prev_ref[b:b + 1, :] = y[(b + 1) * C - 1:(b + 1) * C, :]
    y = y + (shifted - y) * mu_ref[...]
    r_all = y[:, 0:W]
    k_all = y[:, W:2 * W]
    v_all = y[:, 2 * W:3 * W]
    wdad = y[:, 3 * W:3 * W + LANES]
    gd = y[:, 3 * W + LANES:3 * W + 2 * LANES]
    w_all = -_softplus(-(w0_ref[...] + _dot(jnp.tanh(wdad), w2_ref[...]))) - 0.5
    ld_all = -jnp.exp(w_all)
    a_all = _sigmoid(a0_ref[...] + _dot(wdad, a2_ref[...]))
    gate_all = _dot(_sigmoid(gd), g2_ref[...])

    lane = _iota((1, LANES), 1)
    m0 = (lane < RWKV_HEAD_DIM).astype(F32)
    m1 = 1.0 - m0
    seg = (_iota((LANES, LANES), 0) < RWKV_HEAD_DIM) == (_iota((LANES, LANES), 1) < RWKV_HEAD_DIM)
    seg_ones = seg.astype(BF16)
    ri = _iota((2 * C, 2 * C), 0)
    ci = _iota((2 * C, 2 * C), 1)
    same = (ri < C) == (ci < C)
    strict = same & (ci < ri)
    incl = same & (ci <= ri)
    eye = (ri == ci).astype(F32)
    rr, cc = _iota((R, R), 0), _iota((R, R), 1)
    shift = int(np.log2(C))
    tri = ((rr >= cc) & (lax.shift_right_logical(rr, shift) == lax.shift_right_logical(cc, shift))).astype(BF16)

    def stack(x):
        return jnp.concatenate([x * m0, x * m1], axis=0)

    ch = [(b, slice(b * C, (b + 1) * C), slice(p * LANES, (p + 1) * LANES))
          for b in range(B) for p in range(RWKV_PAIRS)]
    P = range(len(ch))
    r = [r_all[rs, s] for _, rs, s in ch]
    v = [v_all[rs, s] for _, rs, s in ch]
    a = [a_all[rs, s] for _, rs, s in ch]
    kk = [k_all[rs, s] * kk_ref[:, s] for _, rs, s in ch]
    nrm = [jnp.sqrt(_dot_l2(x * x, seg_ones)) for x in kk]
    kk = [kk[p] / jnp.maximum(nrm[p], 1e-12) for p in P]
    k = [k_all[ch[p][1], ch[p][2]] * (1.0 + (a[p] - 1.0) * ka_ref[:, ch[p][2]]) for p in P]
    beta = [kk[p] * a[p] for p in P]

    cum_all = _dot_r2(tri, ld_all)
    cum = [cum_all[rs, s] for _, rs, s in ch]
    c_last = [x[C - 1:C, :] for x in cum]
    at = [-kk[p] * jnp.exp(cum[p] - ld_all[ch[p][1], ch[p][2]]) for p in P]
    rt = [r[p] * jnp.exp(cum[p]) for p in P]
    pinv = [jnp.exp(-x) for x in cum]
    bt = [beta[p] * pinv[p] for p in P]
    kt = [k[p] * pinv[p] for p in P]
    dlast = [jnp.exp(c_last[p] - cum[p]) for p in P]
    ar2 = [jnp.concatenate([stack(at[p]), stack(rt[p])], axis=0) for p in P]
    bk2 = [jnp.concatenate([bt[p], bt[p], kt[p], kt[p]], axis=0) for p in P]
    v2 = [stack(x) for x in v]
    g = [st_ref[p] for p in P]

    sc = [_dot(ar2[p], bk2[p], NT) for p in P]
    a_ab = [jnp.where(strict, x[0:2 * C, 0:2 * C], 0.0) for x in sc]
    a_ak = [jnp.where(strict, x[0:2 * C, 2 * C:4 * C], 0.0) for x in sc]
    l_rbk = [jnp.concatenate([jnp.where(incl, x[2 * C:4 * C, 0:2 * C], 0.0),
                              jnp.where(incl, x[2 * C:4 * C, 2 * C:4 * C], 0.0)], axis=1) for x in sc]
    x0 = [_dot(ar2[p], g[p], NT) for p in P]
    rhs = [x0[p][0:2 * C, :] + _dot(a_ak[p], v2[p]) for p in P]
    t = [eye + x for x in a_ab]
    m = [_dot(x, x) for x in a_ab]
    for _ in range(int(np.log2(C)) - 2):
        tm = [_dot(jnp.concatenate([t[p], m[p]], axis=0), m[p]) for p in P]
        t = [t[p] + tm[p][0:2 * C, :] for p in P]
        m = [x[2 * C:4 * C, :] for x in tm]
    t = [t[p] + _dot(t[p], m[p]) for p in P]
    u2 = [_dot(t[p], rhs[p]) for p in P]
    uv = [jnp.concatenate([u2[p], v2[p]], axis=0) for p in P]
    o2 = [x0[p][2 * C:4 * C, :] + _dot(l_rbk[p], uv[p]) for p in P]
    o = [x[0:C, :] + x[C:2 * C, :] for x in o2]
    bkd = [jnp.concatenate([stack(beta[p] * dlast[p]), stack(k[p] * dlast[p])], axis=0) for p in P]
    upd = [_dot(jnp.concatenate([u2[p].T, v2[p].T], axis=1), bkd[p]) for p in P]
    for p in P:
        st_ref[p] = g[p] * jnp.exp(c_last[p]) + jnp.where(seg, upd[p], 0.0)

    mean = [_dot_l2(x, seg_ones) * (1.0 / RWKV_HEAD_DIM) for x in o]
    oc = [o[p] - mean[p] for p in P]
    var = [_dot_l2(x * x, seg_ones) * (1.0 / RWKV_HEAD_DIM) for x in oc]
    bonus = [_dot_l2(r[p] * k[p] * rk_ref[:, ch[p][2]], seg_ones) * v[p] for p in P]
    for p in P:
        b, rs, s = ch[p]
        on = oc[p] * lax.rsqrt(var[p] + RWKV_LNX_EPS) * lg_ref[:, s] + lb_ref[:, s]
        o_ref[b, :, s] = ((on + bonus[p]) * gate_all[rs, s]).astype(o_ref.dtype)


def _rwkv(proj, mu, w0, w2p, a0, a2p, g2, kk, ka, rk, lg, lb, B, S):
    T = B * S
    C = RWKV_CHUNK
    nc = S // C
    W = RWKV_WIDTH
    row = lambda n: pl.BlockSpec((1, n), lambda i: (0, 0))
    mat = lambda: pl.BlockSpec((LANES, W), lambda i: (0, 0))
    out = pl.pallas_call(
        _rwkv_kernel,
        grid=(nc,),
        in_specs=[pl.BlockSpec((B, C, RWKV_IN), lambda i: (0, i, COL_RWKV // RWKV_IN)),
                  row(RWKV_IN), row(W), mat(), row(W), mat(), mat(), row(W), row(W), row(W), row(W), row(W)],
        out_specs=pl.BlockSpec((B, C, W), lambda i: (0, i, 0)),
        out_shape=jax.ShapeDtypeStruct((B, S, W), BF16),
        scratch_shapes=[pltpu.VMEM((B * RWKV_PAIRS, LANES, LANES), F32),
                        pltpu.VMEM((B, RWKV_IN), F32)],
        compiler_params=_cp(("arbitrary",)),
        name="rwkv7_chunked",
    )(proj.reshape(B, S, proj.shape[1]), mu, w0, w2p, a0, a2p, g2, kk, ka, rk, lg, lb)
    return out.reshape(T, W)


def _take_top(works, n):
    works = list(works)
    rows = _iota(works[0].shape, 0)
    n_rows = works[0].shape[0]
    ranks = [jnp.full(w.shape, float(n), F32) for w in works]
    vals = [[] for _ in works]
    for i in range(n):
        m = [jnp.max(w, axis=0, keepdims=True) for w in works]
        first = [jnp.min(jnp.where(w == mm, rows, n_rows), axis=0, keepdims=True) for w, mm in zip(works, m)]
        hit = [rows == f for f in first]
        works = [jnp.where(h, -jnp.inf, w) for h, w in zip(hit, works)]
        ranks = [jnp.where(h, float(i), r) for h, r in zip(hit, ranks)]
        for v, mm in zip(vals, m):
            v.append(mm)
    return vals, ranks


def _batcher_network(n):
    pairs = []
    p = 1
    while p < n:
        k = p
        while k >= 1:
            for j in range(k % p, n - k, 2 * k):
                for i in range(min(k, n - j - k)):
                    if (i + j) // (2 * p) == (i + j + k) // (2 * p):
                        pairs.append((i + j, i + j + k))
            k //= 2
        p *= 2
    return pairs


def _sublane_all(x, op):
    for shift in (4, 2, 1):
        x = op(x, pltpu.roll(x, shift, 0))
    return x


def _sorted_top(pieces):
    p = list(pieces)
    n = len(p)
    for i, j in _batcher_network(n):
        p[i], p[j] = jnp.maximum(p[i], p[j]), jnp.minimum(p[i], p[j])
    for shift in (4, 2, 1):
        p = [jnp.maximum(p[i], pltpu.roll(p[n - 1 - i], shift, 0)) for i in range(n)]
        k = n // 2
        while k >= 1:
            for i in range(n):
                if not i & k:
                    p[i], p[i + k] = jnp.maximum(p[i], p[i + k]), jnp.minimum(p[i], p[i + k])
            k //= 2
    return p


def _peer_pairs():
    K = PEER_TOPK
    pairs = [(a, b) for a in range(K) for b in range(K) if (a + 1) * (b + 1) <= K]
    n_cand = len(pairs) + (-len(pairs) % 8)
    a_idx = _iota((K, n_cand), 0)
    c_idx = _iota((K, n_cand), 1)
    owner = jnp.zeros((K, n_cand), F32)
    for a in range(K):
        first = pairs.index((a, 0))
        owner = jnp.where((a_idx == a) & (c_idx >= first) & (c_idx < first + K // (a + 1)), 1.0, owner)
    return pairs, n_cand, owner


def _peer_score_kernel(q_ref, k1_ref, k2_ref, n1_ref, c1_ref, r2_ref, e2_ref):
    K = PEER_TOPK
    SUB = 8
    H = range(PEER_HEADS)
    lanes = q_ref.shape[0]
    n_pieces = PEER_NKEYS // SUB
    pairs, n_cand, owner = _peer_pairs()
    sub = _iota((SUB, lanes), 0)
    neg = jnp.full((SUB, lanes), -jnp.inf, F32)
    bads = []

    def scores(h):
        base = h * 2 * PEER_HALF
        return (_dot(k1_ref[h], q_ref[:, base:base + PEER_HALF], NT),
                _dot(k2_ref[h], q_ref[:, base + PEER_HALF:base + 2 * PEER_HALF], NT))

    def ambiguous(top, pieces):
        amb = jnp.zeros((SUB, lanes), F32)
        for a in range(len(top) - 1):
            amb = jnp.where(top[a] > top[a + 1], amb, 1.0)
        cnt = jnp.zeros((SUB, lanes), F32)
        for p in pieces:
            cnt = cnt + (p >= top[-1]).astype(F32)
        return jnp.where(_sublane_all(cnt, jnp.add) == len(top), amb, 1.0)

    for h in H:
        s1, s2 = scores(h)
        p1 = [s1[r * SUB:(r + 1) * SUB, :] for r in range(n_pieces)]
        p2 = [s2[r * SUB:(r + 1) * SUB, :] for r in range(n_pieces)]
        t1, t2 = _sorted_top(p1), _sorted_top(p2)
        bad = jnp.maximum(ambiguous(t1, p1), ambiguous(t2, p2))
        cand = []
        for k in range(n_cand // SUB):
            x = neg
            for j, (a, b) in enumerate(pairs[k * SUB:(k + 1) * SUB]):
                x = jnp.where(sub == j, t1[a] + t2[b], x)
            cand.append(x)
        best, work = [], list(cand)
        for _ in range(K):
            m = work[0]
            for w in work[1:]:
                m = jnp.maximum(m, w)
            m = _sublane_all(m, jnp.maximum)
            best.append(m)
            work = [jnp.where(w == m, neg, w) for w in work]
        bads.append(jnp.maximum(bad, ambiguous(best, cand)))
        z = jnp.ones((SUB, lanes), F32)
        for c in best[1:]:
            z = z + jnp.exp(c - best[0])
        sel = jnp.concatenate([(c >= best[-1]).astype(F32) for c in cand], axis=0)
        n_sel = _dot(owner, sel)
        theta = jnp.full((K, lanes), -jnp.inf, F32)
        for b in range(K):
            theta = jnp.where(n_sel == b + 1, -jnp.concatenate([t2[b]] * (K // SUB), axis=0), theta)
        th = [jnp.broadcast_to(theta[a:a + 1, :], (SUB, lanes)) for a in range(K)]
        for r in range(n_pieces):
            rs = slice(r * SUB, (r + 1) * SUB)
            m1 = neg
            for a in range(K):
                m1 = jnp.where(p1[r] == t1[a], th[a], m1)
            n1_ref[h, rs, :] = m1
            c1_ref[h, rs, :] = jnp.exp(p1[r] - t1[0]) / z
            r2_ref[h, rs, :] = -p2[r]
            e2_ref[h, rs, :] = jnp.exp(p2[r] - t2[0])

    def exact_order(h):
        s1, s2 = scores(h)
        (va, vb), (ra, rb) = _take_top([s1, s2], K)
        pad = [jnp.full_like(va[0], -jnp.inf)] * (n_cand - len(pairs))
        (best,), (crank,) = _take_top([jnp.concatenate([va[a] + vb[b] for a, b in pairs] + pad, axis=0)], K)
        z = jnp.ones_like(best[0])
        for c in best[1:]:
            z = z + jnp.exp(c - best[0])
        n_sel = _dot(owner, (crank < K).astype(F32))
        n1 = jnp.full_like(ra, -1.0)
        for a in range(K):
            n1 = jnp.where(ra == a, n_sel[a:a + 1, :] - 1.0, n1)
        n1_ref[h] = n1
        c1_ref[h] = jnp.exp(s1 - va[0]) / z
        r2_ref[h] = rb
        e2_ref[h] = jnp.exp(s2 - vb[0])

    any_bad = functools.reduce(jnp.maximum, bads)

    @pl.when(jnp.max(any_bad) > 0.0)
    def _():
        for h in H:
            pl.when(jnp.max(bads[h]) > 0.0)(functools.partial(exact_order, h))


def _peer_scores(q, k1, k2, tt):
    T = q.shape[0]
    H, NK = PEER_HEADS, PEER_NKEYS
    out = jax.ShapeDtypeStruct((H, NK, T), F32)
    ospec = pl.BlockSpec((H, NK, tt), lambda i: (0, 0, i))
    return pl.pallas_call(
        _peer_score_kernel,
        grid=(T // tt,),
        in_specs=[pl.BlockSpec((tt, q.shape[1]), lambda i: (i, 0)),
                  pl.BlockSpec(k1.shape, lambda i: (0, 0, 0)),
                  pl.BlockSpec(k2.shape, lambda i: (0, 0, 0))],
        out_specs=[ospec, ospec, ospec, ospec],
        out_shape=[out, out, out, out],
        compiler_params=_cp(("parallel",)),
        name="peer_scores",
    )(q, k1, k2)


def _gelu(x):
    return 0.5 * x * (1.0 + lax.erf(x * (2.0 ** -0.5)))


def _peer_dense_kernel(h_ref, u_ref, vt_ref, n1_ref, c1_ref, r2_ref, e2_ref,
                       o_ref, ht_ref, coeff_a_ref, coeff_b_ref, *group_refs, n_i1, blocks_per_tile):
    s = pl.program_id(0)
    NK = PEER_NKEYS
    n_groups = len(group_refs) // 2
    act_refs, acc_refs = group_refs[:n_groups], group_refs[n_groups:]
    rows_a = act_refs[0].shape[0]
    rows_c = acc_refs[0].shape[0]
    per_group = n_i1 // n_groups

    @pl.when(s == 0)
    def _():
        coeff_b_ref[...] = jnp.zeros(coeff_b_ref.shape, BF16)

    @pl.when(s % blocks_per_tile == 0)
    def _():
        ht_ref[...] = h_ref[...].astype(F32).T.astype(BF16)

    @pl.when((s == 0) | (s % blocks_per_tile == 1))
    def _():
        for acc_ref in acc_refs:
            acc_ref[...] = jnp.zeros(acc_ref.shape, F32)

    def activations(g):
        act_refs[g][...] = jnp.dot(u_ref[g * rows_a:(g + 1) * rows_a, :], ht_ref[...], preferred_element_type=F32)

    def accumulate(g, coeff_prev_ref):
        acc_refs[g][...] += jnp.dot(vt_ref[g * rows_c:(g + 1) * rows_c, :], coeff_prev_ref[...],
                                    preferred_element_type=F32)

    def coefficients(il, coeff_cur_ref):
        w = None
        for h in range(PEER_HEADS):
            n1 = n1_ref[h, il:il + 1, :]
            c1 = c1_ref[h, il:il + 1, :]
            term = jnp.where(r2_ref[h] <= n1, e2_ref[h] * c1, 0.0)
            w = term if w is None else w + term
        r0 = (il % per_group) * NK
        act = act_refs[il // per_group][r0:r0 + NK, :]
        coeff_cur_ref[il * NK:(il + 1) * NK, :] = (w * _gelu(act)).astype(BF16)

    def body(coeff_cur_ref, coeff_prev_ref):
        activations(0)
        for g in range(1, n_groups + 1):
            if g < n_groups:
                activations(g)
            accumulate(g - 1, coeff_prev_ref)
            for il in range((g - 1) * per_group, g * per_group):
                coefficients(il, coeff_cur_ref)

    @pl.when(s % 2 == 0)
    def _():
        body(coeff_a_ref, coeff_b_ref)

    @pl.when(s % 2 == 1)
    def _():
        body(coeff_b_ref, coeff_a_ref)

    @pl.when((s > 0) & (s % blocks_per_tile == 0))
    def _():
        for g, acc_ref in enumerate(acc_refs):
            o_ref[:, g * rows_c:(g + 1) * rows_c] = acc_ref[...].T


def _peer_dense(h, u, vt, n1, c1, r2, e2, tm, te):
    T, D = h.shape
    H, NK = PEER_HEADS, PEER_NKEYS
    n_i1 = te // NK
    bpt = PEER_N // te
    last = (T // tm) * bpt - 1
    cur = lambda s: jnp.minimum(s, last)
    prev = lambda s: jnp.maximum(s - 1, 0)
    row_spec = pl.BlockSpec((H, n_i1, tm), lambda s: (0, cur(s) % bpt, cur(s) // bpt))
    all_spec = pl.BlockSpec((H, NK, tm), lambda s: (0, 0, cur(s) // bpt))
    return pl.pallas_call(
        functools.partial(_peer_dense_kernel, n_i1=n_i1, blocks_per_tile=bpt),
        grid=(last + 2,),
        in_specs=[pl.BlockSpec((tm, D), lambda s: (cur(s) // bpt, 0)),
                  pl.BlockSpec((te, D), lambda s: (cur(s) % bpt, 0)),
                  pl.BlockSpec((D, te), lambda s: (0, prev(s) % bpt)),
                  row_spec, row_spec, all_spec, all_spec],
        out_specs=pl.BlockSpec((tm, D), lambda s: (prev(s) // bpt, 0)),
        out_shape=jax.ShapeDtypeStruct((T, D), F32),
        scratch_shapes=([pltpu.VMEM((D, tm), BF16), pltpu.VMEM((te, tm), BF16), pltpu.VMEM((te, tm), BF16)]
                        + [pltpu.VMEM((te // PEER_DENSE_GROUPS, tm), F32)] * PEER_DENSE_GROUPS
                        + [pltpu.VMEM((D // PEER_DENSE_GROUPS, tm), F32)] * PEER_DENSE_GROUPS),
        compiler_params=_cp(("arbitrary",)),
        name="peer_dense",
    )(h, u, vt, n1, c1, r2, e2)


def _pad_heads(w, heads, dim, dim_pad):
    lead = w.shape[:-1]
    w = w.reshape(lead + (heads, dim))
    w = jnp.pad(w, [(0, 0)] * len(lead) + [(0, 0), (0, dim_pad - dim)])
    return w.reshape(lead + (heads * dim_pad,))


def _layout_w_in(w_in):
    pool = w_in[..., 0:512]
    q = _pad_heads(w_in[..., 512:896], GLA_HEADS, GLA_DK, GLA_DK_PAD)
    k = _pad_heads(w_in[..., 896:1280], GLA_HEADS, GLA_DK, GLA_DK_PAD)
    v = _pad_heads(w_in[..., 1280:2048], GLA_HEADS, GLA_DV, GLA_DV_PAD)
    g = _pad_heads(w_in[..., 2048:2816], GLA_HEADS, GLA_DV, GLA_DV_PAD)
    ad = jnp.pad(w_in[..., 2816:2832], [(0, 0)] * (w_in.ndim - 1) + [(0, LANES - GLA_LORA)])
    rw = w_in[..., 2832:5392]
    return jnp.concatenate([rw, pool, q, k, v, g, ad], axis=-1).astype(BF16)


def kernel(x, c, ada_w, ada_b, norm1_g, w_in, pool_w, pool_scale, gla_alpha_up, gla_alpha_b, gla_norm_g, rwkv_mu, rwkv_w0, rwkv_w2, rwkv_a0, rwkv_a2, rwkv_g2, rwkv_kk, rwkv_ka, rwkv_rk, rwkv_lnx_g, rwkv_lnx_b, w_out, norm2_g, peer_wq, peer_k1, peer_k2, peer_u, peer_v, final_g):
    B, S, D = x.shape
    L = ada_w.shape[0]
    T = B * S
    tm = min(512, S)
    tpb = S // tm
    tm_mm = min(1024, S)
    tn_mm = 1024
    W = RWKV_WIDTH

    mod = _modulation(c, ada_w, ada_b)
    x2 = x.reshape(T, D)
    ffn = None
    for l in range(L):
        mod3 = mod[l].reshape(B, 1, 6 * D)
        if l == 0:
            h = _normmod(x2, norm1_g[l], mod3, 0, 1, tpb, tm)
        else:
            x2, h = _resid_normmod(x2, ffn, mod[l - 1].reshape(B, 1, 6 * D), 5, norm1_g[l], mod3, 0, 1, tpb, tm)
        proj = _matmul(h, _layout_w_in(w_in[l]), tm_mm, 896)
        y_pool = _pool(proj, pool_w[l], pool_scale[l], B, S, tm)
        up_p = jnp.pad(_pad_heads(gla_alpha_up[l], GLA_HEADS, GLA_DK, GLA_DK_PAD),
                       ((0, LANES - GLA_LORA), (0, 0)))
        ab_p = _pad_heads(gla_alpha_b[l], GLA_HEADS, GLA_DK, GLA_DK_PAD).reshape(1, -1)
        ng_p = _pad_heads(gla_norm_g[l], GLA_HEADS, GLA_DV, GLA_DV_PAD).reshape(1, -1)
        y_gla = _gla(proj, up_p, ab_p, ng_p, B, S)
        zeros = jnp.zeros((RWKV_HEAD_DIM, W), F32)
        w2p = jnp.concatenate([rwkv_w2[l], zeros], axis=0).astype(BF16)
        a2p = jnp.concatenate([zeros, rwkv_a2[l]], axis=0).astype(BF16)
        y_rwkv = _rwkv(proj, rwkv_mu[l].reshape(1, -1), rwkv_w0[l].reshape(1, W), w2p,
                       rwkv_a0[l].reshape(1, W), a2p, rwkv_g2[l].astype(BF16),
                       rwkv_kk[l].reshape(1, W), rwkv_ka[l].reshape(1, W), rwkv_rk[l].reshape(1, W),
                       rwkv_lnx_g[l].reshape(1, W), rwkv_lnx_b[l].reshape(1, W), B, S)
        wo = w_out[l]
        wo_g = jnp.pad(wo[512:1280].reshape(GLA_HEADS, GLA_DV, D),
                       ((0, 0), (0, GLA_DV_PAD - GLA_DV), (0, 0))).reshape(GLA_V_PAD, D)
        x2 = _outproj(y_pool, y_gla, y_rwkv, wo[0:512].astype(BF16), wo_g.astype(BF16),
                      wo[1280:].astype(BF16), x2, mod3, 2 * D // tn_mm, S // tm_mm, tm_mm, tn_mm)
        h = _normmod(x2, norm2_g[l], mod3, 3, 4, tpb, tm)
        q = _matmul(h, peer_wq[l].astype(BF16), tm_mm, tn_mm)
        n1, c1, r2, e2 = _peer_scores(q, peer_k1[l], peer_k2[l], 2 * LANES)
        ffn = _peer_dense(h, peer_u[l].astype(BF16), peer_v[l].T.astype(BF16), n1, c1, r2, e2, tm, 1024)
    out = _resid_final(x2, ffn, mod[L - 1].reshape(B, 1, 6 * D), 5, final_g, tpb, tm)
    return out.reshape(B, S, D)
```

```python
import functools

import jax
import jax.numpy as jnp
import numpy as np
from jax import lax
from jax.experimental import pallas as pl
from jax.experimental.pallas import tpu as pltpu

F32 = jnp.float32
BF16 = jnp.bfloat16

LANES = 128
NORM_EPS = 1e-6

POOL_WINDOWS = (2, 4, 8, 16)
POOL_GROUP_DIM = 128
POOL_WIDTH = 512
POOL_HALO = 16

GLA_HEADS = 4
GLA_DK = 96
GLA_DV = 192
GLA_DK_PAD = 128
GLA_DV_PAD = 256
GLA_LORA = 16
GLA_TAU = 16.0
GLA_CHUNK = 64
GLA_SUB = 16
GLA_QK_PAD = GLA_HEADS * GLA_DK_PAD
GLA_V_PAD = GLA_HEADS * GLA_DV_PAD

RWKV_WIDTH = 768
RWKV_HEAD_DIM = 64
RWKV_PAIRS = RWKV_WIDTH // LANES
RWKV_IN = 2560
RWKV_CHUNK = 64
RWKV_LNX_EPS = 64e-5

COL_RWKV = 0
COL_POOL = 2560
COL_Q = 3072
COL_K = 3584
COL_V = 4096
COL_G = 5120
COL_AD = 6144
IN_PAD = 6272

PEER_HEADS = 8
PEER_NKEYS = 128
PEER_HALF = 128
PEER_TOPK = 16
PEER_N = PEER_NKEYS * PEER_NKEYS
PEER_DENSE_GROUPS = 4

NN = (((1,), (0,)), ((), ()))
NT = (((1,), (1,)), ((), ()))

VMEM_LIMIT = 56 * 1024 * 1024


def _cp(sem):
    return pltpu.CompilerParams(dimension_semantics=sem, vmem_limit_bytes=VMEM_LIMIT)


def _dot(a, b, dims=NN):
    return lax.dot_general(a.astype(BF16), b.astype(BF16), dims, preferred_element_type=F32)


def _split(x):
    hi = x.astype(BF16)
    lo = (x - hi.astype(F32)).astype(BF16)
    return hi, lo


def _dot_l2(a, b, dims=NN):
    hi, lo = _split(a)
    bb = b.astype(BF16)
    return (lax.dot_general(hi, bb, dims, preferred_element_type=F32)
            + lax.dot_general(lo, bb, dims, preferred_element_type=F32))


def _dot_r2(a, b):
    hi, lo = _split(b)
    ab = a.astype(BF16)
    return (lax.dot_general(ab, hi, NN, preferred_element_type=F32)
            + lax.dot_general(ab, lo, NN, preferred_element_type=F32))


def _iota(shape, dim):
    return lax.broadcasted_iota(jnp.int32, shape, dim)


def _tri_incl(n):
    return (_iota((n, n), 0) >= _iota((n, n), 1)).astype(BF16)


def _sigmoid(x):
    return 1.0 / (1.0 + jnp.exp(-x))


def _softplus(x):
    return jnp.maximum(x, 0.0) + jnp.log1p(jnp.exp(-jnp.abs(x)))


def _mod_kernel(c_ref, w_ref, b_ref, o_ref):
    c = c_ref[...]
    ca = c * _sigmoid(c)
    o_ref[...] = jnp.dot(ca, w_ref[...], precision=lax.Precision.HIGHEST,
                         preferred_element_type=F32) + b_ref[...]


def _modulation(c, ada_w, ada_b):
    L, D, N6 = ada_w.shape
    B = c.shape[0]
    tn = 1024
    return pl.pallas_call(
        _mod_kernel,
        grid=(L, N6 // tn),
        in_specs=[pl.BlockSpec((B, D), lambda l, j: (0, 0)),
                  pl.BlockSpec((None, D, tn), lambda l, j: (l, 0, j)),
                  pl.BlockSpec((None, 1, tn), lambda l, j: (l, 0, j))],
        out_specs=pl.BlockSpec((None, B, tn), lambda l, j: (l, 0, j)),
        out_shape=jax.ShapeDtypeStruct((L, B, N6), F32),
        compiler_params=_cp(("parallel", "parallel")),
        name="adaln_mod",
    )(c, ada_w, ada_b.reshape(L, 1, N6))


def _normmod_kernel(x_ref, g_ref, sh_ref, sc_ref, o_ref):
    x = x_ref[...]
    y = x * lax.rsqrt(jnp.mean(x * x, axis=-1, keepdims=True) + NORM_EPS) * g_ref[...]
    o_ref[...] = (y * (1.0 + sc_ref[...]) + sh_ref[...]).astype(o_ref.dtype)


def _normmod(x2, g, mod3, sh_blk, sc_blk, tiles_per_batch, tm):
    T, D = x2.shape
    return pl.pallas_call(
        _normmod_kernel,
        grid=(T // tm,),
        in_specs=[pl.BlockSpec((tm, D), lambda i: (i, 0)),
                  pl.BlockSpec((1, D), lambda i: (0, 0)),
                  pl.BlockSpec((None, 1, D), lambda i: (i // tiles_per_batch, 0, sh_blk)),
                  pl.BlockSpec((None, 1, D), lambda i: (i // tiles_per_batch, 0, sc_blk))],
        out_specs=pl.BlockSpec((tm, D), lambda i: (i, 0)),
        out_shape=jax.ShapeDtypeStruct((T, D), BF16),
        compiler_params=_cp(("parallel",)),
        name="rmsnorm_adaln",
    )(x2, g.reshape(1, D), mod3, mod3)


def _resid_normmod_kernel(x_ref, f_ref, gate_ref, g_ref, sh_ref, sc_ref, xo_ref, h_ref):
    x = x_ref[...] + gate_ref[...] * f_ref[...]
    xo_ref[...] = x
    y = x * lax.rsqrt(jnp.mean(x * x, axis=-1, keepdims=True) + NORM_EPS) * g_ref[...]
    h_ref[...] = (y * (1.0 + sc_ref[...]) + sh_ref[...]).astype(h_ref.dtype)


def _resid_normmod(x2, f, mod_prev3, gate_blk, g, mod3, sh_blk, sc_blk, tiles_per_batch, tm):
    T, D = x2.shape
    row = lambda blk: pl.BlockSpec((None, 1, D), lambda i: (i // tiles_per_batch, 0, blk))
    tile = pl.BlockSpec((tm, D), lambda i: (i, 0))
    return pl.pallas_call(
        _resid_normmod_kernel,
        grid=(T // tm,),
        in_specs=[tile, tile, row(gate_blk), pl.BlockSpec((1, D), lambda i: (0, 0)), row(sh_blk), row(sc_blk)],
        out_specs=[tile, tile],
        out_shape=[jax.ShapeDtypeStruct((T, D), F32), jax.ShapeDtypeStruct((T, D), BF16)],
        compiler_params=_cp(("parallel",)),
        name="residual_rmsnorm_adaln",
    )(x2, f, mod_prev3, g.reshape(1, D), mod3, mod3)


def _resid_final_kernel(x_ref, f_ref, gate_ref, g_ref, o_ref):
    x = x_ref[...] + gate_ref[...] * f_ref[...]
    o_ref[...] = x * lax.rsqrt(jnp.mean(x * x, axis=-1, keepdims=True) + NORM_EPS) * g_ref[...]


def _resid_final(x2, f, mod_prev3, gate_blk, g, tiles_per_batch, tm):
    T, D = x2.shape
    tile = pl.BlockSpec((tm, D), lambda i: (i, 0))
    return pl.pallas_call(
        _resid_final_kernel,
        grid=(T // tm,),
        in_specs=[tile, tile,
                  pl.BlockSpec((None, 1, D), lambda i: (i // tiles_per_batch, 0, gate_blk)),
                  pl.BlockSpec((1, D), lambda i: (0, 0))],
        out_specs=tile,
        out_shape=jax.ShapeDtypeStruct((T, D), F32),
        compiler_params=_cp(("parallel",)),
        name="residual_final_rmsnorm",
    )(x2, f, mod_prev3, g.reshape(1, D))


def _matmul_kernel(a_ref, w_ref, o_ref):
    o_ref[...] = jnp.dot(a_ref[...], w_ref[...], preferred_element_type=F32)


def _matmul(a, w, tm, tn):
    M, K = a.shape
    N = w.shape[1]
    return pl.pallas_call(
        _matmul_kernel,
        grid=(M // tm, N // tn),
        in_specs=[pl.BlockSpec((tm, K), lambda i, j: (i, 0)),
                  pl.BlockSpec((K, tn), lambda i, j: (0, j))],
        out_specs=pl.BlockSpec((tm, tn), lambda i, j: (i, j)),
        out_shape=jax.ShapeDtypeStruct((M, N), F32),
        compiler_params=_cp(("parallel", "parallel")),
        name="projection",
    )(a, w)


def _outproj_kernel(yp_ref, yg_ref, yr_ref, wp_ref, wg_ref, wr_ref, x_ref, gate_ref, o_ref):
    acc = jnp.dot(yp_ref[...], wp_ref[...], preferred_element_type=F32)
    acc += jnp.dot(yg_ref[...], wg_ref[...], preferred_element_type=F32)
    acc += jnp.dot(yr_ref[...], wr_ref[...], preferred_element_type=F32)
    o_ref[...] = x_ref[...] + gate_ref[...] * acc


def _outproj(yp, yg, yr, wp, wg, wr, x2, mod3, gate_blk0, tiles_per_batch, tm, tn):
    T, D = x2.shape
    return pl.pallas_call(
        _outproj_kernel,
        grid=(T // tm, D // tn),
        in_specs=[pl.BlockSpec((tm, yp.shape[1]), lambda i, j: (i, 0)),
                  pl.BlockSpec((tm, yg.shape[1]), lambda i, j: (i, 0)),
                  pl.BlockSpec((tm, yr.shape[1]), lambda i, j: (i, 0)),
                  pl.BlockSpec((wp.shape[0], tn), lambda i, j: (0, j)),
                  pl.BlockSpec((wg.shape[0], tn), lambda i, j: (0, j)),
                  pl.BlockSpec((wr.shape[0], tn), lambda i, j: (0, j)),
                  pl.BlockSpec((tm, tn), lambda i, j: (i, j)),
                  pl.BlockSpec((None, 1, tn), lambda i, j: (i // tiles_per_batch, 0, gate_blk0 + j))],
        out_specs=pl.BlockSpec((tm, tn), lambda i, j: (i, j)),
        out_shape=jax.ShapeDtypeStruct((T, D), F32),
        compiler_params=_cp(("parallel", "parallel")),
        name="out_projection",
    )(yp, yg, yr, wp, wg, wr, x2, mod3)


def _pool_kernel(p_ref, w_ref, scale_ref, o_ref, ext_ref, *, ts):
    i = pl.program_id(1)

    @pl.when(i == 0)
    def _():
        ext_ref[0:POOL_HALO, :] = jnp.zeros((POOL_HALO, POOL_WIDTH), F32)

    p = p_ref[...]
    ext_ref[POOL_HALO:POOL_HALO + ts, :] = p
    pos = (i * ts + _iota((ts, 1), 0) + 1).astype(F32)
    for g, win in enumerate(POOL_WINDOWS):
        lo = g * POOL_GROUP_DIM
        acc = p[:, lo:lo + POOL_GROUP_DIM]
        for j in range(1, win):
            acc = acc + ext_ref[POOL_HALO - j:POOL_HALO - j + ts, lo:lo + POOL_GROUP_DIM]
        d = acc / jnp.minimum(pos, float(win)) - p[:, lo:lo + POOL_GROUP_DIM]
        y = _dot(d, w_ref[g]) * scale_ref[:, lo:lo + POOL_GROUP_DIM]
        o_ref[:, lo:lo + POOL_GROUP_DIM] = y.astype(o_ref.dtype)
    ext_ref[0:POOL_HALO, :] = ext_ref[ts:ts + POOL_HALO, :]


def _pool(proj, pool_w, pool_scale, B, S, ts):
    T = B * S
    nt = S // ts
    return pl.pallas_call(
        functools.partial(_pool_kernel, ts=ts),
        grid=(B, nt),
        in_specs=[pl.BlockSpec((ts, POOL_WIDTH), lambda b, i: (b * nt + i, COL_POOL // POOL_WIDTH)),
                  pl.BlockSpec(pool_w.shape, lambda b, i: (0, 0, 0)),
                  pl.BlockSpec((1, POOL_WIDTH), lambda b, i: (0, 0))],
        out_specs=pl.BlockSpec((ts, POOL_WIDTH), lambda b, i: (b * nt + i, 0)),
        out_shape=jax.ShapeDtypeStruct((T, POOL_WIDTH), BF16),
        scratch_shapes=[pltpu.VMEM((ts + POOL_HALO, POOL_WIDTH), F32)],
        compiler_params=_cp(("arbitrary", "arbitrary")),
        name="causal_pool",
    )(proj, pool_w, pool_scale.reshape(1, POOL_WIDTH))


def _gla_kernel(q_ref, k_ref, v_ref, g_ref, ad_ref, up_ref, ab_ref, ng_ref, o_ref, st_ref):
    C, SUB = GLA_CHUNK, GLA_SUB

    B = q_ref.shape[0]
    R = B * C

    @pl.when(pl.program_id(0) == 0)
    def _():
        st_ref[...] = jnp.zeros(st_ref.shape, F32)

    ad = jnp.concatenate([ad_ref[b] for b in range(B)], axis=0)
    z = _dot(ad, up_ref[...]) + ab_ref[...]
    la = -_softplus(-z) * (1.0 / GLA_TAU)
    rr, cc = _iota((R, R), 0), _iota((R, R), 1)
    sh = int(np.log2(C))
    tri = ((rr >= cc) & (lax.shift_right_logical(rr, sh) == lax.shift_right_logical(cc, sh))).astype(BF16)
    bcum = _dot_r2(tri, la)
    rowc = _iota((C, 1), 0)
    causal = _iota((C, C), 1) <= _iota((C, C), 0)
    scale = GLA_DK ** -0.5
    subs = range(C // SUB)
    seq = [s for s in range(B) for _ in range(GLA_HEADS)]
    H = range(len(seq))
    ks = [slice(h * GLA_DK_PAD, (h + 1) * GLA_DK_PAD) for _ in range(B) for h in range(GLA_HEADS)]
    vs = [slice(h * GLA_DV_PAD, (h + 1) * GLA_DV_PAD) for _ in range(B) for h in range(GLA_HEADS)]
    q = [q_ref[seq[h], :, ks[h]] * scale for h in H]
    k = [k_ref[seq[h], :, ks[h]] for h in H]
    v = [v_ref[seq[h], :, vs[h]] for h in H]
    b = [bcum[seq[h] * C:(seq[h] + 1) * C, ks[h]] for h in H]
    st = [st_ref[h] for h in H]
    refs = [[b[h][s * SUB - 1:s * SUB, :] if s else jnp.zeros((1, GLA_DK_PAD), F32) for s in subs] for h in H]
    in_sub = [(rowc >= s * SUB) & (rowc < (s + 1) * SUB) for s in subs]
    q_bd = [jnp.concatenate([jnp.where(in_sub[s], q[h] * jnp.exp(b[h] - refs[h][s]), 0.0) for s in subs], axis=1)
            for h in H]
    k_cat = [jnp.concatenate([k[h] * jnp.exp(jnp.where(rowc < (s + 1) * SUB, refs[h][s] - b[h], 0.0)) for s in subs],
                             axis=1) for h in H]
    att = [jnp.where(causal, _dot(q_bd[h], k_cat[h], NT), 0.0) for h in H]
    out = [_dot(q[h] * jnp.exp(b[h]), st[h], NT) + _dot(att[h], v[h]) for h in H]
    b_last = [x[C - 1:C, :] for x in b]
    zpad_v = jnp.zeros((GLA_DV_PAD - C, GLA_DV_PAD), F32)
    zpad_k = jnp.zeros((GLA_DV_PAD - C, GLA_DK_PAD), F32)
    upd = [_dot(jnp.concatenate([v[h], zpad_v], axis=0).T,
                jnp.concatenate([k[h] * jnp.exp(b_last[h] - b[h]), zpad_k], axis=0)) for h in H]
    for h in H:
        st_ref[h] = st[h] * jnp.exp(b_last[h]) + upd[h]
    for h in H:
        ms = jnp.sum(out[h] * out[h], axis=-1, keepdims=True) * (1.0 / GLA_DV)
        o = out[h] * lax.rsqrt(ms + NORM_EPS) * ng_ref[:, vs[h]]
        gg = g_ref[seq[h], :, vs[h]]
        o_ref[seq[h], :, vs[h]] = (o * (gg * _sigmoid(gg))).astype(o_ref.dtype)


def _gla(proj, up_p, ab_p, ng_p, B, S):
    T = B * S
    C = GLA_CHUNK
    nc = S // C
    p3 = proj.reshape(B, S, proj.shape[1])
    out = pl.pallas_call(
        _gla_kernel,
        grid=(nc,),
        in_specs=[pl.BlockSpec((B, C, GLA_QK_PAD), lambda i: (0, i, COL_Q // GLA_QK_PAD)),
                  pl.BlockSpec((B, C, GLA_QK_PAD), lambda i: (0, i, COL_K // GLA_QK_PAD)),
                  pl.BlockSpec((B, C, GLA_V_PAD), lambda i: (0, i, COL_V // GLA_V_PAD)),
                  pl.BlockSpec((B, C, GLA_V_PAD), lambda i: (0, i, COL_G // GLA_V_PAD)),
                  pl.BlockSpec((B, C, LANES), lambda i: (0, i, COL_AD // LANES)),
                  pl.BlockSpec((LANES, GLA_QK_PAD), lambda i: (0, 0)),
                  pl.BlockSpec((1, GLA_QK_PAD), lambda i: (0, 0)),
                  pl.BlockSpec((1, GLA_V_PAD), lambda i: (0, 0))],
        out_specs=pl.BlockSpec((B, C, GLA_V_PAD), lambda i: (0, i, 0)),
        out_shape=jax.ShapeDtypeStruct((B, S, GLA_V_PAD), BF16),
        scratch_shapes=[pltpu.VMEM((B * GLA_HEADS, GLA_DV_PAD, GLA_DK_PAD), F32)],
        compiler_params=_cp(("arbitrary",)),
        name="gla_chunked",
    )(p3, p3, p3, p3, p3, up_p, ab_p, ng_p)
    return out.reshape(T, GLA_V_PAD)


def _rwkv_kernel(y_ref, mu_ref, w0_ref, w2_ref, a0_ref, a2_ref, g2_ref, kk_ref, ka_ref, rk_ref,
                 lg_ref, lb_ref, o_ref, st_ref, prev_ref):
    C = RWKV_CHUNK
    W = RWKV_WIDTH

    B = y_ref.shape[0]
    R = B * C

    @pl.when(pl.program_id(0) == 0)
    def _():
        st_ref[...] = jnp.zeros(st_ref.shape, F32)
        prev_ref[...] = jnp.zeros(prev_ref.shape, F32)

    y = jnp.concatenate([y_ref[b] for b in range(B)], axis=0)
    row = _iota((R, 1), 0)
    prev = prev_ref[B - 1:B, :]
    first = row == (B - 1) * C
    for b in range(B - 2, -1, -1):
        prev = jnp.where(row < (b + 1) * C, prev_ref[b:b + 1, :], prev)
        first = first | (row == b * C)
    shifted = jnp.where(first, prev, pltpu.roll(y, 1, 0))
    for b in range(B):
        prev_ref[b:b + 1, :] = y[(b + 1) * C - 1:(b + 1) * C, :]
    y = y + (shifted - y) * mu_ref[...]
    r_all = y[:, 0:W]
    k_all = y[:, W:2 * W]
    v_all = y[:, 2 * W:3 * W]
    wdad = y[:, 3 * W:3 * W + LANES]
    gd = y[:, 3 * W + LANES:3 * W + 2 * LANES]
    w_all = -_softplus(-(w0_ref[...] + _dot(jnp.tanh(wdad), w2_ref[...]))) - 0.5
    ld_all = -jnp.exp(w_all)
    a_all = _sigmoid(a0_ref[...] + _dot(wdad, a2_ref[...]))
    gate_all = _dot(_sigmoid(gd), g2_ref[...])

    lane = _iota((1, LANES), 1)
    m0 = (lane < RWKV_HEAD_DIM).astype(F32)
    m1 = 1.0 - m0
    seg = (_iota((LANES, LANES), 0) < RWKV_HEAD_DIM) == (_iota((LANES, LANES), 1) < RWKV_HEAD_DIM)
    seg_ones = seg.astype(BF16)
    ri = _iota((2 * C, 2 * C), 0)
    ci = _iota((2 * C, 2 * C), 1)
    same = (ri < C) == (ci < C)
    strict = same & (ci < ri)
    incl = same & (ci <= ri)
    eye = (ri == ci).astype(F32)
    rr, cc = _iota((R, R), 0), _iota((R, R), 1)
    shift = int(np.log2(C))
    tri = ((rr >= cc) & (lax.shift_right_logical(rr, shift) == lax.shift_right_logical(cc, shift))).astype(BF16)

    def stack(x):
        return jnp.concatenate([x * m0, x * m1], axis=0)

    ch = [(b, slice(b * C, (b + 1) * C), slice(p * LANES, (p + 1) * LANES))
          for b in range(B) for p in range(RWKV_PAIRS)]
    P = range(len(ch))
    r = [r_all[rs, s] for _, rs, s in ch]
    v = [v_all[rs, s] for _, rs, s in ch]
    a = [a_all[rs, s] for _, rs, s in ch]
    kk = [k_all[rs, s] * kk_ref[:, s] for _, rs, s in ch]
    nrm = [jnp.sqrt(_dot_l2(x * x, seg_ones)) for x in kk]
    kk = [kk[p] / jnp.maximum(nrm[p], 1e-12) for p in P]
    k = [k_all[ch[p][1], ch[p][2]] * (1.0 + (a[p] - 1.0) * ka_ref[:, ch[p][2]]) for p in P]
    beta = [kk[p] * a[p] for p in P]

    cum_all = _dot_r2(tri, ld_all)
    cum = [cum_all[rs, s] for _, rs, s in ch]
    c_last = [x[C - 1:C, :] for x in cum]
    at = [-kk[p] * jnp.exp(cum[p] - ld_all[ch[p][1], ch[p][2]]) for p in P]
    rt = [r[p] * jnp.exp(cum[p]) for p in P]
    pinv = [jnp.exp(-x) for x in cum]
    bt = [beta[p] * pinv[p] for p in P]
    kt = [k[p] * pinv[p] for p in P]
    dlast = [jnp.exp(c_last[p] - cum[p]) for p in P]
    ar2 = [jnp.concatenate([stack(at[p]), stack(rt[p])], axis=0) for p in P]
    bk2 = [jnp.concatenate([bt[p], bt[p], kt[p], kt[p]], axis=0) for p in P]
    v2 = [stack(x) for x in v]
    g = [st_ref[p] for p in P]

    sc = [_dot(ar2[p], bk2[p], NT) for p in P]
    a_ab = [jnp.where(strict, x[0:2 * C, 0:2 * C], 0.0) for x in sc]
    a_ak = [jnp.where(strict, x[0:2 * C, 2 * C:4 * C], 0.0) for x in sc]
    l_rbk = [jnp.concatenate([jnp.where(incl, x[2 * C:4 * C, 0:2 * C], 0.0),
                              jnp.where(incl, x[2 * C:4 * C, 2 * C:4 * C], 0.0)], axis=1) for x in sc]
    x0 = [_dot(ar2[p], g[p], NT) for p in P]
    rhs = [x0[p][0:2 * C, :] + _dot(a_ak[p], v2[p]) for p in P]
    t = [eye + x for x in a_ab]
    m = [_dot(x, x) for x in a_ab]
    for _ in range(int(np.log2(C)) - 2):
        tm = [_dot(jnp.concatenate([t[p], m[p]], axis=0), m[p]) for p in P]
        t = [t[p] + tm[p][0:2 * C, :] for p in P]
        m = [x[2 * C:4 * C, :] for x in tm]
    t = [t[p] + _dot(t[p], m[p]) for p in P]
    u2 = [_dot(t[p], rhs[p]) for p in P]
    uv = [jnp.concatenate([u2[p], v2[p]], axis=0) for p in P]
    o2 = [x0[p][2 * C:4 * C, :] + _dot(l_rbk[p], uv[p]) for p in P]
    o = [x[0:C, :] + x[C:2 * C, :] for x in o2]
    bkd = [jnp.concatenate([stack(beta[p] * dlast[p]), stack(k[p] * dlast[p])], axis=0) for p in P]
    upd = [_dot(jnp.concatenate([u2[p].T, v2[p].T], axis=1), bkd[p]) for p in P]
    for p in P:
        st_ref[p] = g[p] * jnp.exp(c_last[p]) + jnp.where(seg, upd[p], 0.0)

    mean = [_dot_l2(x, seg_ones) * (1.0 / RWKV_HEAD_DIM) for x in o]
    oc = [o[p] - mean[p] for p in P]
    var = [_dot_l2(x * x, seg_ones) * (1.0 / RWKV_HEAD_DIM) for x in oc]
    bonus = [_dot_l2(r[p] * k[p] * rk_ref[:, ch[p][2]], seg_ones) * v[p] for p in P]
    for p in P:
        b, rs, s = ch[p]
        on = oc[p] * lax.rsqrt(var[p] + RWKV_LNX_EPS) * lg_ref[:, s] + lb_ref[:, s]
        o_ref[b, :, s] = ((on + bonus[p]) * gate_all[rs, s]).astype(o_ref.dtype)


def _rwkv(proj, mu, w0, w2p, a0, a2p, g2, kk, ka, rk, lg, lb, B, S):
    T = B * S
    C = RWKV_CHUNK
    nc = S // C
    W = RWKV_WIDTH
    row = lambda n: pl.BlockSpec((1, n), lambda i: (0, 0))
    mat = lambda: pl.BlockSpec((LANES, W), lambda i: (0, 0))
    out = pl.pallas_call(
        _rwkv_kernel,
        grid=(nc,),
        in_specs=[pl.BlockSpec((B, C, RWKV_IN), lambda i: (0, i, COL_RWKV // RWKV_IN)),
                  row(RWKV_IN), row(W), mat(), row(W), mat(), mat(), row(W), row(W), row(W), row(W), row(W)],
        out_specs=pl.BlockSpec((B, C, W), lambda i: (0, i, 0)),
        out_shape=jax.ShapeDtypeStruct((B, S, W), BF16),
        scratch_shapes=[pltpu.VMEM((B * RWKV_PAIRS, LANES, LANES), F32),
                        pltpu.VMEM((B, RWKV_IN), F32)],
        compiler_params=_cp(("arbitrary",)),
        name="rwkv7_chunked",
    )(proj.reshape(B, S, proj.shape[1]), mu, w0, w2p, a0, a2p, g2, kk, ka, rk, lg, lb)
    return out.reshape(T, W)


def _take_top(works, n):
    works = list(works)
    rows = _iota(works[0].shape, 0)
    n_rows = works[0].shape[0]
    ranks = [jnp.full(w.shape, float(n), F32) for w in works]
    vals = [[] for _ in works]
    for i in range(n):
        m = [jnp.max(w, axis=0, keepdims=True) for w in works]
        first = [jnp.min(jnp.where(w == mm, rows, n_rows), axis=0, keepdims=True) for w, mm in zip(works, m)]
        hit = [rows == f for f in first]
        works = [jnp.where(h, -jnp.inf, w) for h, w in zip(hit, works)]
        ranks = [jnp.where(h, float(i), r) for h, r in zip(hit, ranks)]
        for v, mm in zip(vals, m):
            v.append(mm)
    return vals, ranks


def _batcher_network(n):
    pairs = []
    p = 1
    while p < n:
        k = p
        while k >= 1:
            for j in range(k % p, n - k, 2 * k):
                for i in range(min(k, n - j - k)):
                    if (i + j) // (2 * p) == (i + j + k) // (2 * p):
                        pairs.append((i + j, i + j + k))
            k //= 2
        p *= 2
    return pairs


def _sublane_all(x, op):
    for shift in (4, 2, 1):
        x = op(x, pltpu.roll(x, shift, 0))
    return x


def _sorted_top(pieces):
    p = list(pieces)
    n = len(p)
    for i, j in _batcher_network(n):
        p[i], p[j] = jnp.maximum(p[i], p[j]), jnp.minimum(p[i], p[j])
    for shift in (4, 2, 1):
        p = [jnp.maximum(p[i], pltpu.roll(p[n - 1 - i], shift, 0)) for i in range(n)]
        k = n // 2
        while k >= 1:
            for i in range(n):
                if not i & k:
                    p[i], p[i + k] = jnp.maximum(p[i], p[i + k]), jnp.minimum(p[i], p[i + k])
            k //= 2
    return p


def _peer_pairs():
    K = PEER_TOPK
    pairs = [(a, b) for a in range(K) for b in range(K) if (a + 1) * (b + 1) <= K]
    n_cand = len(pairs) + (-len(pairs) % 8)
    a_idx = _iota((K, n_cand), 0)
    c_idx = _iota((K, n_cand), 1)
    owner = jnp.zeros((K, n_cand), F32)
    for a in range(K):
        first = pairs.index((a, 0))
        owner = jnp.where((a_idx == a) & (c_idx >= first) & (c_idx < first + K // (a + 1)), 1.0, owner)
    return pairs, n_cand, owner


def _peer_score_kernel(q_ref, k1_ref, k2_ref, n1_ref, c1_ref, r2_ref, e2_ref):
    K = PEER_TOPK
    SUB = 8
    H = range(PEER_HEADS)
    lanes = q_ref.shape[0]
    n_pieces = PEER_NKEYS // SUB
    pairs, n_cand, owner = _peer_pairs()
    sub = _iota((SUB, lanes), 0)
    neg = jnp.full((SUB, lanes), -jnp.inf, F32)
    bads = []

    def scores(h):
        base = h * 2 * PEER_HALF
        return (_dot(k1_ref[h], q_ref[:, base:base + PEER_HALF], NT),
                _dot(k2_ref[h], q_ref[:, base + PEER_HALF:base + 2 * PEER_HALF], NT))

    def ambiguous(top, pieces):
        amb = jnp.zeros((SUB, lanes), F32)
        for a in range(len(top) - 1):
            amb = jnp.where(top[a] > top[a + 1], amb, 1.0)
        cnt = jnp.zeros((SUB, lanes), F32)
        for p in pieces:
            cnt = cnt + (p >= top[-1]).astype(F32)
        return jnp.where(_sublane_all(cnt, jnp.add) == len(top), amb, 1.0)

    for h in H:
        s1, s2 = scores(h)
        p1 = [s1[r * SUB:(r + 1) * SUB, :] for r in range(n_pieces)]
        p2 = [s2[r * SUB:(r + 1) * SUB, :] for r in range(n_pieces)]
        t1, t2 = _sorted_top(p1), _sorted_top(p2)
        bad = jnp.maximum(ambiguous(t1, p1), ambiguous(t2, p2))
        cand = []
        for k in range(n_cand // SUB):
            x = neg
            for j, (a, b) in enumerate(pairs[k * SUB:(k + 1) * SUB]):
                x = jnp.where(sub == j, t1[a] + t2[b], x)
            cand.append(x)
        best, work = [], list(cand)
        for _ in range(K):
            m = work[0]
            for w in work[1:]:
                m = jnp.maximum(m, w)
            m = _sublane_all(m, jnp.maximum)
            best.append(m)
            work = [jnp.where(w == m, neg, w) for w in work]
        bads.append(jnp.maximum(bad, ambiguous(best, cand)))
        z = jnp.ones((SUB, lanes), F32)
        for c in best[1:]:
            z = z + jnp.exp(c - best[0])
        sel = jnp.concatenate([(c >= best[-1]).astype(F32) for c in cand], axis=0)
        n_sel = _dot(owner, sel)
        theta = jnp.full((K, lanes), -jnp.inf, F32)
        for b in range(K):
            theta = jnp.where(n_sel == b + 1, -jnp.concatenate([t2[b]] * (K // SUB), axis=0), theta)
        th = [jnp.broadcast_to(theta[a:a + 1, :], (SUB, lanes)) for a in range(K)]
        for r in range(n_pieces):
            rs = slice(r * SUB, (r + 1) * SUB)
            m1 = neg
            for a in range(K):
                m1 = jnp.where(p1[r] == t1[a], th[a], m1)
            n1_ref[h, rs, :] = m1
            c1_ref[h, rs, :] = jnp.exp(p1[r] - t1[0]) / z
            r2_ref[h, rs, :] = -p2[r]
            e2_ref[h, rs, :] = jnp.exp(p2[r] - t2[0])

    def exact_order(h):
        s1, s2 = scores(h)
        (va, vb), (ra, rb) = _take_top([s1, s2], K)
        pad = [jnp.full_like(va[0], -jnp.inf)] * (n_cand - len(pairs))
        (best,), (crank,) = _take_top([jnp.concatenate([va[a] + vb[b] for a, b in pairs] + pad, axis=0)], K)
        z = jnp.ones_like(best[0])
        for c in best[1:]:
            z = z + jnp.exp(c - best[0])
        n_sel = _dot(owner, (crank < K).astype(F32))
        n1 = jnp.full_like(ra, -1.0)
        for a in range(K):
            n1 = jnp.where(ra == a, n_sel[a:a + 1, :] - 1.0, n1)
        n1_ref[h] = n1
        c1_ref[h] = jnp.exp(s1 - va[0]) / z
        r2_ref[h] = rb
        e2_ref[h] = jnp.exp(s2 - vb[0])

    any_bad = functools.reduce(jnp.maximum, bads)

    @pl.when(jnp.max(any_bad) > 0.0)
    def _():
        for h in H:
            pl.when(jnp.max(bads[h]) > 0.0)(functools.partial(exact_order, h))


def _peer_scores(q, k1, k2, tt):
    T = q.shape[0]
    H, NK = PEER_HEADS, PEER_NKEYS
    out = jax.ShapeDtypeStruct((H, NK, T), F32)
    ospec = pl.BlockSpec((H, NK, tt), lambda i: (0, 0, i))
    return pl.pallas_call(
        _peer_score_kernel,
        grid=(T // tt,),
        in_specs=[pl.BlockSpec((tt, q.shape[1]), lambda i: (i, 0)),
                  pl.BlockSpec(k1.shape, lambda i: (0, 0, 0)),
                  pl.BlockSpec(k2.shape, lambda i: (0, 0, 0))],
        out_specs=[ospec, ospec, ospec, ospec],
        out_shape=[out, out, out, out],
        compiler_params=_cp(("parallel",)),
        name="peer_scores",
    )(q, k1, k2)


def _gelu(x):
    return 0.5 * x * (1.0 + lax.erf(x * (2.0 ** -0.5)))


def _peer_dense_kernel(h_ref, u_ref, vt_ref, n1_ref, c1_ref, r2_ref, e2_ref,
                       o_ref, ht_ref, coeff_a_ref, coeff_b_ref, *group_refs, n_i1, blocks_per_tile):
    s = pl.program_id(0)
    NK = PEER_NKEYS
    n_groups = len(group_refs) // 2
    act_refs, acc_refs = group_refs[:n_groups], group_refs[n_groups:]
    rows_a = act_refs[0].shape[0]
    rows_c = acc_refs[0].shape[0]
    per_group = n_i1 // n_groups

    @pl.when(s == 0)
    def _():
        coeff_b_ref[...] = jnp.zeros(coeff_b_ref.shape, BF16)

    @pl.when(s % blocks_per_tile == 0)
    def _():
        ht_ref[...] = h_ref[...].astype(F32).T.astype(BF16)

    @pl.when((s == 0) | (s % blocks_per_tile == 1))
    def _():
        for acc_ref in acc_refs:
            acc_ref[...] = jnp.zeros(acc_ref.shape, F32)

    def activations(g):
        act_refs[g][...] = jnp.dot(u_ref[g * rows_a:(g + 1) * rows_a, :], ht_ref[...], preferred_element_type=F32)

    def accumulate(g, coeff_prev_ref):
        acc_refs[g][...] += jnp.dot(vt_ref[g * rows_c:(g + 1) * rows_c, :], coeff_prev_ref[...],
                                    preferred_element_type=F32)

    def coefficients(il, coeff_cur_ref):
        w = None
        for h in range(PEER_HEADS):
            n1 = n1_ref[h, il:il + 1, :]
            c1 = c1_ref[h, il:il + 1, :]
            term = jnp.where(r2_ref[h] <= n1, e2_ref[h] * c1, 0.0)
            w = term if w is None else w + term
        r0 = (il % per_group) * NK
        act = act_refs[il // per_group][r0:r0 + NK, :]
        coeff_cur_ref[il * NK:(il + 1) * NK, :] = (w * _gelu(act)).astype(BF16)

    def body(coeff_cur_ref, coeff_prev_ref):
        activations(0)
        for g in range(1, n_groups + 1):
            if g < n_groups:
                activations(g)
            accumulate(g - 1, coeff_prev_ref)
            for il in range((g - 1) * per_group, g * per_group):
                coefficients(il, coeff_cur_ref)

    @pl.when(s % 2 == 0)
    def _():
        body(coeff_a_ref, coeff_b_ref)

    @pl.when(s % 2 == 1)
    def _():
        body(coeff_b_ref, coeff_a_ref)

    @pl.when((s > 0) & (s % blocks_per_tile == 0))
    def _():
        for g, acc_ref in enumerate(acc_refs):
            o_ref[:, g * rows_c:(g + 1) * rows_c] = acc_ref[...].T


def _peer_dense(h, u, vt, n1, c1, r2, e2, tm, te):
    T, D = h.shape
    H, NK = PEER_HEADS, PEER_NKEYS
    n_i1 = te // NK
    bpt = PEER_N // te
    last = (T // tm) * bpt - 1
    cur = lambda s: jnp.minimum(s, last)
    prev = lambda s: jnp.maximum(s - 1, 0)
    row_spec = pl.BlockSpec((H, n_i1, tm), lambda s: (0, cur(s) % bpt, cur(s) // bpt))
    all_spec = pl.BlockSpec((H, NK, tm), lambda s: (0, 0, cur(s) // bpt))
    return pl.pallas_call(
        functools.partial(_peer_dense_kernel, n_i1=n_i1, blocks_per_tile=bpt),
        grid=(last + 2,),
        in_specs=[pl.BlockSpec((tm, D), lambda s: (cur(s) // bpt, 0)),
                  pl.BlockSpec((te, D), lambda s: (cur(s) % bpt, 0)),
                  pl.BlockSpec((D, te), lambda s: (0, prev(s) % bpt)),
                  row_spec, row_spec, all_spec, all_spec],
        out_specs=pl.BlockSpec((tm, D), lambda s: (prev(s) // bpt, 0)),
        out_shape=jax.ShapeDtypeStruct((T, D), F32),
        scratch_shapes=([pltpu.VMEM((D, tm), BF16), pltpu.VMEM((te, tm), BF16), pltpu.VMEM((te, tm), BF16)]
                        + [pltpu.VMEM((te // PEER_DENSE_GROUPS, tm), F32)] * PEER_DENSE_GROUPS
                        + [pltpu.VMEM((D // PEER_DENSE_GROUPS, tm), F32)] * PEER_DENSE_GROUPS),
        compiler_params=_cp(("arbitrary",)),
        name="peer_dense",
    )(h, u, vt, n1, c1, r2, e2)


def _pad_heads(w, heads, dim, dim_pad):
    lead = w.shape[:-1]
    w = w.reshape(lead + (heads, dim))
    w = jnp.pad(w, [(0, 0)] * len(lead) + [(0, 0), (0, dim_pad - dim)])
    return w.reshape(lead + (heads * dim_pad,))


def _layout_w_in(w_in):
    pool = w_in[..., 0:512]
    q = _pad_heads(w_in[..., 512:896], GLA_HEADS, GLA_DK, GLA_DK_PAD)
    k = _pad_heads(w_in[..., 896:1280], GLA_HEADS, GLA_DK, GLA_DK_PAD)
    v = _pad_heads(w_in[..., 1280:2048], GLA_HEADS, GLA_DV, GLA_DV_PAD)
    g = _pad_heads(w_in[..., 2048:2816], GLA_HEADS, GLA_DV, GLA_DV_PAD)
    ad = jnp.pad(w_in[..., 2816:2832], [(0, 0)] * (w_in.ndim - 1) + [(0, LANES - GLA_LORA)])
    rw = w_in[..., 2832:5392]
    return jnp.concatenate([rw, pool, q, k, v, g, ad], axis=-1).astype(BF16)


def kernel(x, c, ada_w, ada_b, norm1_g, w_in, pool_w, pool_scale, gla_alpha_up, gla_alpha_b, gla_norm_g, rwkv_mu, rwkv_w0, rwkv_w2, rwkv_a0, rwkv_a2, rwkv_g2, rwkv_kk, rwkv_ka, rwkv_rk, rwkv_lnx_g, rwkv_lnx_b, w_out, norm2_g, peer_wq, peer_k1, peer_k2, peer_u, peer_v, final_g):
    B, S, D = x.shape
    L = ada_w.shape[0]
    T = B * S
    tm = min(512, S)
    tpb = S // tm
    tm_mm = min(1024, S)
    tn_mm = 1024
    W = RWKV_WIDTH

    mod = _modulation(c, ada_w, ada_b)
    x2 = x.reshape(T, D)
    ffn = None
    for l in range(L):
        mod3 = mod[l].reshape(B, 1, 6 * D)
        if l == 0:
            h = _normmod(x2, norm1_g[l], mod3, 0, 1, tpb, tm)
        else:
            x2, h = _resid_normmod(x2, ffn, mod[l - 1].reshape(B, 1, 6 * D), 5, norm1_g[l], mod3, 0, 1, tpb, tm)
        proj = _matmul(h, _layout_w_in(w_in[l]), tm_mm, 896)
        y_pool = _pool(proj, pool_w[l], pool_scale[l], B, S, tm)
        up_p = jnp.pad(_pad_heads(gla_alpha_up[l], GLA_HEADS, GLA_DK, GLA_DK_PAD),
                       ((0, LANES - GLA_LORA), (0, 0)))
        ab_p = _pad_heads(gla_alpha_b[l], GLA_HEADS, GLA_DK, GLA_DK_PAD).reshape(1, -1)
        ng_p = _pad_heads(gla_norm_g[l], GLA_HEADS, GLA_DV, GLA_DV_PAD).reshape(1, -1)
        y_gla = _gla(proj, up_p, ab_p, ng_p, B, S)
        zeros = jnp.zeros((RWKV_HEAD_DIM, W), F32)
        w2p = jnp.concatenate([rwkv_w2[l], zeros], axis=0).astype(BF16)
        a2p = jnp.concatenate([zeros, rwkv_a2[l]], axis=0).astype(BF16)
        y_rwkv = _rwkv(proj, rwkv_mu[l].reshape(1, -1), rwkv_w0[l].reshape(1, W), w2p,
                       rwkv_a0[l].reshape(1, W), a2p, rwkv_g2[l].astype(BF16),
                       rwkv_kk[l].reshape(1, W), rwkv_ka[l].reshape(1, W), rwkv_rk[l].reshape(1, W),
                       rwkv_lnx_g[l].reshape(1, W), rwkv_lnx_b[l].reshape(1, W), B, S)
        wo = w_out[l]
        wo_g = jnp.pad(wo[512:1280].reshape(GLA_HEADS, GLA_DV, D),
                       ((0, 0), (0, GLA_DV_PAD - GLA_DV), (0, 0))).reshape(GLA_V_PAD, D)
        x2 = _outproj(y_pool, y_gla, y_rwkv, wo[0:512].astype(BF16), wo_g.astype(BF16),
                      wo[1280:].astype(BF16), x2, mod3, 2 * D // tn_mm, S // tm_mm, tm_mm, tn_mm)
        h = _normmod(x2, norm2_g[l], mod3, 3, 4, tpb, tm)
        q = _matmul(h, peer_wq[l].astype(BF16), tm_mm, tn_mm)
        n1, c1, r2, e2 = _peer_scores(q, peer_k1[l], peer_k2[l], 2 * LANES)
        ffn = _peer_dense(h, peer_u[l].astype(BF16), peer_v[l].T.astype(BF16), n1, c1, r2, e2, tm, 1024)
    out = _resid_final(x2, ffn, mod[L - 1].reshape(B, 1, 6 * D), 5, final_g, tpb, tm)
    return out.reshape(B, S, D)
```
